```python
import math
import jax, jax.numpy as jnp
from jax import lax
import numpy as np

D_MODEL = 1024
BATCH = 4
SEQ = 4096
DEPTH = 1
DEC_BATCH = 128
DEC_SEQ = 1
PAST_LEN = 16384
PAGE_SIZE = 128

N_META = 16
Q_BLOCK = 128
MLA_HEADS = 8
MLA_Q_RANK = 256
MLA_KV_RANK = 256
MLA_NOPE = 64
MLA_ROPE = 32
MLA_V = 64
MLA_SCALE = (MLA_NOPE + MLA_ROPE) ** -0.5
DIFF_HEADS = 8
DIFF_DIM = 32
DIFF_V = 2 * DIFF_DIM
DIFF_SCALE = DIFF_DIM ** -0.5
D_FF = 4 * D_MODEL
ROPE_THETA = 10000.0
LN_EPS = 1e-5
RMS_EPS = 1e-6
DEEPNORM_ALPHA = (2 * DEPTH) ** 0.25
DEEPNORM_BETA = (8 * DEPTH) ** -0.25
F32 = jnp.float32

_SIZES = (MLA_Q_RANK, MLA_KV_RANK, MLA_ROPE, DIFF_HEADS * 2 * DIFF_DIM,
          DIFF_HEADS * 2 * DIFF_DIM, DIFF_HEADS * DIFF_V, 2 * D_MODEL)
SPLITS = tuple(int(v) for v in np.cumsum(_SIZES)[:-1])
D_IN = int(sum(_SIZES))

kernel_name = 'mla_diffattn_gated_hybrid_step'


def layer_norm(x, g, b):
    xf = x.astype(F32)
    mu = jnp.mean(xf, -1, keepdims=True)
    var = jnp.mean(jnp.square(xf - mu), -1, keepdims=True)
    return ((xf - mu) * lax.rsqrt(var + LN_EPS) * g + b).astype(x.dtype)


def rms_norm(x, g):
    xf = x.astype(F32)
    return (xf * lax.rsqrt(jnp.mean(jnp.square(xf), -1, keepdims=True) + RMS_EPS) * g).astype(x.dtype)


def rope(x, pos):
    half = x.shape[-1] // 2
    inv = ROPE_THETA ** (-jnp.arange(half, dtype=F32) / half)
    ang = pos.astype(F32)[:, None] * inv[None, :]
    c = jnp.cos(ang)[:, None, :]
    s = jnp.sin(ang)[:, None, :]
    xf = x.astype(F32)
    x1, x2 = xf[..., :half], xf[..., half:]
    return jnp.concatenate([x1 * c - x2 * s, x1 * s + x2 * c], -1).astype(x.dtype)


def token_front(x, pos, p):
    B, T = x.shape[:2]
    h = x @ p['w_in']
    c_q, c_kv, k_r, dq, dk, dv, g = jnp.split(h, SPLITS, axis=-1)
    q = (rms_norm(c_q, p['mla_q_norm']) @ p['w_uq']).reshape(B, T, MLA_HEADS, MLA_NOPE + MLA_ROPE)
    q_nope = q[..., :MLA_NOPE]
    q_rope = rope(q[..., MLA_NOPE:], pos)
    lat = rms_norm(c_kv, p['mla_kv_norm'])
    k_rope = rope(k_r[:, :, None, :], pos)[:, :, 0]
    q_diff = rope(dq.reshape(B, T, 2 * DIFF_HEADS, DIFF_DIM), pos).reshape(B, T, DIFF_HEADS, 2 * DIFF_DIM)
    k_diff = rope(dk.reshape(B, T, 2 * DIFF_HEADS, DIFF_DIM), pos).reshape(B, T, DIFF_HEADS, 2 * DIFF_DIM)
    v_diff = dv.reshape(B, T, DIFF_HEADS, DIFF_V)
    gates = jax.nn.sigmoid((g + p['b_gate']).astype(F32)).astype(x.dtype)
    return q_nope, q_rope, lat, k_rope, q_diff, k_diff, v_diff, gates


def token_back(x, o_mla, o_diff, gates, lam_init, p):
    B, T = x.shape[:2]
    o_diff = rms_norm(o_diff, p['diff_subln']) * (1.0 - lam_init)
    a = o_mla.reshape(B, T, MLA_HEADS * MLA_V) @ p['w_br_mla']
    b = o_diff.reshape(B, T, DIFF_HEADS * DIFF_V) @ p['w_br_diff']
    mix = (gates[..., :D_MODEL] * a + gates[..., D_MODEL:] * b) @ p['w_out']
    x1 = layer_norm(DEEPNORM_ALPHA * x + mix, p['ln1_g'], p['ln1_b'])
    f = jnp.square(jax.nn.relu(x1 @ p['w_up'])) @ p['w_down']
    return layer_norm(DEEPNORM_ALPHA * x1 + f, p['ln2_g'], p['ln2_b'])


def masked_softmax(s, mask):
    return jax.nn.softmax(jnp.where(mask, s, -jnp.inf), axis=-1)


def attend_block(q_mla, q_diff, q_pos, k_mla, v_mla, k_diff, v_diff, k_pos, lam):
    mask = (q_pos[:, None] >= k_pos[None, :])[None, None]
    s = jnp.einsum('bqhd,bkhd->bhqk', q_mla, k_mla).astype(F32) * MLA_SCALE
    o_mla = jnp.einsum('bhqk,bkhd->bqhd', masked_softmax(s, mask).astype(v_mla.dtype), v_mla)
    s1 = jnp.einsum('bqhd,bkhd->bhqk', q_diff[..., :DIFF_DIM], k_diff[..., :DIFF_DIM]).astype(F32) * DIFF_SCALE
    s2 = jnp.einsum('bqhd,bkhd->bhqk', q_diff[..., DIFF_DIM:], k_diff[..., DIFF_DIM:]).astype(F32) * DIFF_SCALE
    pd = masked_softmax(s1, mask) - lam * masked_softmax(s2, mask)
    o_diff = jnp.einsum('bhqk,bkhd->bqhd', pd.astype(v_diff.dtype), v_diff)
    return o_mla, o_diff


def prompt_attention(q_nope, q_rope, lat, k_rope, q_diff, k_diff, v_diff, lam, w_uk, w_uv):
    B, L = lat.shape[:2]
    k_nope = jnp.einsum('blc,chd->blhd', lat, w_uk)
    v_mla = jnp.einsum('blc,chd->blhd', lat, w_uv)
    k_mla = jnp.concatenate([k_nope, jnp.broadcast_to(k_rope[:, :, None, :], (B, L, MLA_HEADS, MLA_ROPE))], -1)
    q_mla = jnp.concatenate([q_nope, q_rope], -1)
    pos = jnp.arange(L)
    M = N_META
    om_meta, od_meta = attend_block(q_mla[:, :M], q_diff[:, :M], pos[:M], k_mla[:, :M], v_mla[:, :M],
                                    k_diff[:, :M], v_diff[:, :M], pos[:M], lam)
    nb = (L - M) // Q_BLOCK

    def blocks(a):
        return a[:, M:].reshape((B, nb, Q_BLOCK) + a.shape[2:]).swapaxes(0, 1)

    def one_block(args):
        qm, qd, qp = args
        return attend_block(qm, qd, qp, k_mla, v_mla, k_diff, v_diff, pos, lam)

    om_r, od_r = lax.map(one_block, (blocks(q_mla), blocks(q_diff), pos[M:].reshape(nb, Q_BLOCK)))

    def unblock(a):
        return a.swapaxes(0, 1).reshape((B, nb * Q_BLOCK) + a.shape[3:])

    return (jnp.concatenate([om_meta, unblock(om_r)], 1), jnp.concatenate([od_meta, unblock(od_r)], 1))


def _pv(p, v):
    if v.ndim == 3:
        return jnp.einsum('bhqk,bkc->bhqc', p, v.astype(F32))
    return jnp.einsum('bhqk,bkhd->bhqd', p, v.astype(F32))


def online_init(s, v):
    m = jnp.max(s, -1)
    p = jnp.exp(s - m[..., None])
    return (m, jnp.sum(p, -1), _pv(p, v))


def online_update(state, s, v):
    m, l, acc = state
    m_new = jnp.maximum(m, jnp.max(s, -1))
    corr = jnp.exp(m - m_new)
    p = jnp.exp(s - m_new[..., None])
    return (m_new, l * corr + jnp.sum(p, -1), acc * corr[..., None] + _pv(p, v))


def sample_attention(q_nope, q_rope, lat, k_rope, q_diff, k_diff, v_diff, lam, w_uk, w_uv,
                     layer, cache_lat, cache_kr, cache_k, cache_v, page_table):
    T = q_nope.shape[1]
    q_lat = jnp.einsum('bqhd,chd->bqhc', q_nope, w_uk)

    def scores(lat_k, kr_k, kd_k):
        sm = (jnp.einsum('bqhc,bkc->bhqk', q_lat, lat_k)
              + jnp.einsum('bqhr,bkr->bhqk', q_rope, kr_k)).astype(F32) * MLA_SCALE
        s1 = jnp.einsum('bqhd,bkhd->bhqk', q_diff[..., :DIFF_DIM], kd_k[..., :DIFF_DIM]).astype(F32) * DIFF_SCALE
        s2 = jnp.einsum('bqhd,bkhd->bhqk', q_diff[..., DIFF_DIM:], kd_k[..., DIFF_DIM:]).astype(F32) * DIFF_SCALE
        return sm, s1, s2

    mask = jnp.tril(jnp.ones((T, T), bool))[None, None]
    sm, s1, s2 = [jnp.where(mask, s, -jnp.inf) for s in scores(lat, k_rope, k_diff)]
    carry = (online_init(sm, lat), online_init(s1, v_diff), online_init(s2, v_diff))

    def step(c, phys):
        lat_k = cache_lat[layer, phys]
        kr_k = cache_kr[layer, phys]
        kd_k = cache_k[layer, phys]
        vd_k = cache_v[layer, phys]
        a, b1, b2 = scores(lat_k, kr_k, kd_k)
        cm, c1, c2 = c
        return (online_update(cm, a, lat_k), online_update(c1, b1, vd_k), online_update(c2, b2, vd_k)), None

    (cm, c1, c2), _ = lax.scan(step, carry, page_table.T)
    o_lat = cm[2] / cm[1][..., None]
    o_mla = jnp.einsum('bhqc,chd->bqhd', o_lat.astype(q_nope.dtype), w_uv)
    o_diff = (c1[2] / c1[1][..., None] - lam * (c2[2] / c2[1][..., None]))
    o_diff = o_diff.transpose(0, 2, 1, 3).astype(v_diff.dtype)
    return o_mla, o_diff


def setup_inputs(seed: int = 0) -> dict:
    key = jax.random.key(seed)
    ks = jax.random.split(key, 40)
    n_pages = PAST_LEN // PAGE_SIZE
    n_used = DEC_BATCH * n_pages
    n_pool = n_used + n_used // 4
    beta = DEEPNORM_BETA

    def nrm(k, shape, scale):
        return jax.random.normal(k, shape, F32) * scale

    col_scale = jnp.ones((D_IN,), F32).at[SPLITS[4]:SPLITS[5]].set(beta)
    perm = jax.random.permutation(ks[0], n_pool)[:n_used]
    return {
        'x_prompt': nrm(ks[1], (BATCH, SEQ, D_MODEL), 1.0),
        'x_sample': nrm(ks[2], (DEC_BATCH, DEC_SEQ, D_MODEL), 1.0),
        'cache_mla_latent': nrm(ks[3], (DEPTH, n_pool, PAGE_SIZE, MLA_KV_RANK), 1.0),
        'cache_mla_krope': nrm(ks[4], (DEPTH, n_pool, PAGE_SIZE, MLA_ROPE), 1.0),
        'cache_diff_k': nrm(ks[5], (DEPTH, n_pool, PAGE_SIZE, DIFF_HEADS, 2 * DIFF_DIM), 1.0),
        'cache_diff_v': nrm(ks[6], (DEPTH, n_pool, PAGE_SIZE, DIFF_HEADS, DIFF_V), 1.0),
        'page_table': perm.reshape(DEC_BATCH, n_pages).astype(jnp.int32),
        'meta_tokens': nrm(ks[7], (N_META, D_MODEL), 1.0),
        'ln_in_g': 1.0 + nrm(ks[8], (D_MODEL,), 0.02),
        'ln_in_b': nrm(ks[9], (D_MODEL,), 0.02),
        'w_in': nrm(ks[10], (DEPTH, D_MODEL, D_IN), D_MODEL ** -0.5) * col_scale,
        'b_gate': nrm(ks[11], (DEPTH, 2 * D_MODEL), 0.02),
        'mla_q_norm': 1.0 + nrm(ks[12], (DEPTH, MLA_Q_RANK), 0.02),
        'w_uq': nrm(ks[13], (DEPTH, MLA_Q_RANK, MLA_HEADS * (MLA_NOPE + MLA_ROPE)), MLA_Q_RANK ** -0.5),
        'mla_kv_norm': 1.0 + nrm(ks[14], (DEPTH, MLA_KV_RANK), 0.02),
        'w_uk': nrm(ks[15], (DEPTH, MLA_KV_RANK, MLA_HEADS, MLA_NOPE), MLA_KV_RANK ** -0.5),
        'w_uv': nrm(ks[16], (DEPTH, MLA_KV_RANK, MLA_HEADS, MLA_V), beta * MLA_KV_RANK ** -0.5),
        'diff_lambda': nrm(ks[17], (DEPTH, 4, DIFF_DIM), 0.1),
        'diff_subln': 1.0 + nrm(ks[18], (DEPTH, DIFF_V), 0.02),
        'w_br_mla': nrm(ks[19], (DEPTH, MLA_HEADS * MLA_V, D_MODEL), beta * (MLA_HEADS * MLA_V) ** -0.5),
        'w_br_diff': nrm(ks[20], (DEPTH, DIFF_HEADS * DIFF_V, D_MODEL), beta * (DIFF_HEADS * DIFF_V) ** -0.5),
        'w_out': nrm(ks[21], (DEPTH, D_MODEL, D_MODEL), beta * D_MODEL ** -0.5),
        'ln1_g': 1.0 + nrm(ks[22], (DEPTH, D_MODEL), 0.02),
        'ln1_b': nrm(ks[23], (DEPTH, D_MODEL), 0.02),
        'w_up': nrm(ks[24], (DEPTH, D_MODEL, D_FF), beta * D_MODEL ** -0.5),
        'w_down': nrm(ks[25], (DEPTH, D_FF, D_MODEL), beta * D_FF ** -0.5),
        'ln2_g': 1.0 + nrm(ks[26], (DEPTH, D_MODEL), 0.02),
        'ln2_b': nrm(ks[27], (DEPTH, D_MODEL), 0.02),
    }


def reference(x_prompt, x_sample, cache_mla_latent, cache_mla_krope, cache_diff_k, cache_diff_v,
              page_table, meta_tokens, ln_in_g, ln_in_b, w_in, b_gate, mla_q_norm, w_uq,
              mla_kv_norm, w_uk, w_uv, diff_lambda, diff_subln, w_br_mla, w_br_diff, w_out,
              ln1_g, ln1_b, w_up, w_down, ln2_g, ln2_b):
    B = x_prompt.shape[0]
    meta = jnp.broadcast_to(meta_tokens[None].astype(x_prompt.dtype), (B, N_META, D_MODEL))
    hp = layer_norm(jnp.concatenate([meta, x_prompt], 1), ln_in_g, ln_in_b)
    hs = layer_norm(x_sample, ln_in_g, ln_in_b)
    L = hp.shape[1]
    T = hs.shape[1]
    past_len = page_table.shape[1] * PAGE_SIZE
    pos_p = jnp.arange(L)
    pos_s = past_len + jnp.arange(T)
    st_p = ([], [], [], [])
    st_s = ([], [], [], [])
    for layer in range(DEPTH):
        p = dict(w_in=w_in[layer], b_gate=b_gate[layer], mla_q_norm=mla_q_norm[layer], w_uq=w_uq[layer],
                 mla_kv_norm=mla_kv_norm[layer], diff_subln=diff_subln[layer], w_br_mla=w_br_mla[layer],
                 w_br_diff=w_br_diff[layer], w_out=w_out[layer], ln1_g=ln1_g[layer], ln1_b=ln1_b[layer],
                 w_up=w_up[layer], w_down=w_down[layer], ln2_g=ln2_g[layer], ln2_b=ln2_b[layer])
        lam_init = 0.8 - 0.6 * math.exp(-0.3 * layer)
        lv = diff_lambda[layer].astype(F32)
        lam = jnp.exp(jnp.sum(lv[0] * lv[1])) - jnp.exp(jnp.sum(lv[2] * lv[3])) + lam_init

        qn, qr, lat, kr, qd, kd, vd, gates = token_front(hp, pos_p, p)
        om, od = prompt_attention(qn, qr, lat, kr, qd, kd, vd, lam, w_uk[layer], w_uv[layer])
        hp = token_back(hp, om, od, gates, lam_init, p)
        for lst, arr in zip(st_p, (lat, kr, kd, vd)):
            lst.append(arr)

        qn, qr, lat, kr, qd, kd, vd, gates = token_front(hs, pos_s, p)
        om, od = sample_attention(qn, qr, lat, kr, qd, kd, vd, lam, w_uk[layer], w_uv[layer], layer,
                                  cache_mla_latent, cache_mla_krope, cache_diff_k, cache_diff_v, page_table)
        hs = token_back(hs, om, od, gates, lam_init, p)
        for lst, arr in zip(st_s, (lat, kr, kd, vd)):
            lst.append(arr)

    y_prompt = hp[:, N_META:]
    y_sample = hs
    new_p_lat, new_p_kr, new_p_k, new_p_v = [jnp.stack(l, 0) for l in st_p]
    new_s_lat, new_s_kr, new_s_k, new_s_v = [jnp.stack(l, 0) for l in st_s]
    return (y_prompt, y_sample, new_p_lat, new_p_kr, new_p_k, new_p_v, new_s_lat, new_s_kr, new_s_k, new_s_v)
```

```python
import functools
import math

import jax
import jax.numpy as jnp
from jax import lax
from jax.experimental import pallas as pl
from jax.experimental.pallas import tpu as pltpu

F32 = jnp.float32
BF16 = jnp.bfloat16

D_MODEL = 1024
N_META_TOK = 16
PAGE = 128
HEADS = 8
Q_RANK = 256
KV_RANK = 256
NOPE = 64
ROPE = 32
MLA_V = 64
DIFF_DIM = 32
DIFF_V = 64
D_FF = 4 * D_MODEL
DEPTH = 1
ROPE_THETA = 10000.0
LN_EPS = 1e-5
RMS_EPS = 1e-6
MLA_SCALE = (NOPE + ROPE) ** -0.5
DIFF_SCALE = DIFF_DIM ** -0.5
DEEPNORM_ALPHA = (2 * DEPTH) ** 0.25
LAM_INIT = 0.8 - 0.6 * math.exp(-0.3 * 0)

LANES = 128
SUBLANES = 8
SLAB = 128
HALF = DIFF_DIM // 2

C_CQ, C_CKV, C_DQ, C_DK, C_DV, C_G, C_KR, C_END = 0, 256, 512, 1024, 1536, 2048, 4096, 4224

TM = 256
TK = 512
DEC_G = 8
DEC_ROWS = 32
NEG_INF = float("-inf")


def _cparams(sem, vmem_mb):
    return pltpu.CompilerParams(dimension_semantics=sem, vmem_limit_bytes=vmem_mb * 1024 * 1024)


def _dot(a, b):
    return jnp.dot(a, b, preferred_element_type=F32)


def _dot_nt(a, b):
    return lax.dot_general(a, b, (((1,), (1,)), ((), ())), preferred_element_type=F32)


def _layer_norm(x, g, b):
    mu = jnp.mean(x, axis=-1, keepdims=True)
    xc = x - mu
    var = jnp.mean(xc * xc, axis=-1, keepdims=True)
    return xc * lax.rsqrt(var + LN_EPS) * g + b


def _rms_norm(x, g):
    return x * lax.rsqrt(jnp.mean(x * x, axis=-1, keepdims=True) + RMS_EPS) * g


def _rope_slabs(x, c, s_up, s_dn):
    outs = []
    for s in range(x.shape[1] // LANES):
        xs = x[:, LANES * s:LANES * (s + 1)]
        outs.append(xs * c + pltpu.roll(xs, HALF, 1) * s_up + pltpu.roll(xs, LANES - HALF, 1) * s_dn)
    return outs[0] if len(outs) == 1 else jnp.concatenate(outs, axis=1)


def _sub_ln_slab(x, g128):
    lane = lax.broadcasted_iota(jnp.int32, x.shape, 1)
    lo = lane < DIFF_V
    x2 = x * x
    s0 = jnp.sum(jnp.where(lo, x2, 0.0), axis=1, keepdims=True)
    s1 = jnp.sum(jnp.where(lo, 0.0, x2), axis=1, keepdims=True)
    ms = jnp.where(lo, s0, s1) * (1.0 / DIFF_V)
    return x * lax.rsqrt(ms + RMS_EPS) * g128 * (1.0 - LAM_INIT)


def _lambda(dl):
    a = jnp.sum(dl[0:1, :] * dl[1:2, :], axis=1, keepdims=True)
    b = jnp.sum(dl[2:3, :] * dl[3:4, :], axis=1, keepdims=True)
    return jnp.exp(a) - jnp.exp(b) + LAM_INIT


def _front_core(x, cosb, sinb, lng, lnb, win_ref, bg, qn, wuq_ref, kvn, wkv_ref):
    tm = x.shape[0]
    hb = _layer_norm(x, lng, lnb).astype(BF16)

    lane = lax.broadcasted_iota(jnp.int32, (tm, LANES), 1)
    first = (lane & (DIFF_DIM - 1)) < HALF
    zero = jnp.zeros_like(sinb)
    s_up_d = jnp.where(first, zero, sinb)
    s_dn_d = jnp.where(first, -sinb, zero)
    c_q = jnp.where(lane < NOPE, 1.0, cosb)
    s_up_q = jnp.where((lane >= NOPE + HALF) & (lane < NOPE + ROPE), sinb, zero)
    s_dn_q = jnp.where((lane >= NOPE) & (lane < NOPE + HALF), -sinb, zero)
    c_k = jnp.where(lane < ROPE, cosb, zero)
    s_up_k = jnp.where((lane >= HALF) & (lane < ROPE), sinb, zero)
    s_dn_k = jnp.where(lane < HALF, -sinb, zero)

    ya = _dot(hb, win_ref[:, C_CQ:C_DQ])
    cqn = _rms_norm(ya[:, :Q_RANK], qn).astype(BF16)
    q = _rope_slabs(_dot(cqn, wuq_ref[...]), c_q, s_up_q, s_dn_q) * MLA_SCALE
    lat = _rms_norm(ya[:, Q_RANK:], kvn)
    kv = _dot(lat.astype(BF16), wkv_ref[...])
    ykr = _dot(hb, win_ref[:, C_KR:C_END])
    kr = ykr * c_k + pltpu.roll(ykr, HALF, 1) * s_up_k + pltpu.roll(ykr, LANES - HALF, 1) * s_dn_k
    kr_at_rope = pltpu.roll(kr, NOPE, 1)
    k_mla = jnp.concatenate(
        [kv[:, SLAB * s:SLAB * (s + 1)] + kr_at_rope for s in range(HEADS)], axis=1)
    v_mla = kv[:, HEADS * SLAB:]

    yd = _dot(hb, win_ref[:, C_DQ:C_G])
    qd = _rope_slabs(yd[:, 0:512], cosb, s_up_d, s_dn_d) * DIFF_SCALE
    kd = _rope_slabs(yd[:, 512:1024], cosb, s_up_d, s_dn_d)
    vd = yd[:, 1024:1536]
    gates = jax.nn.sigmoid(_dot(hb, win_ref[:, C_G:C_KR]) + bg)
    return dict(q=q, lat=lat, kr=kr, k_mla=k_mla, v_mla=v_mla, qd=qd, kd=kd, vd=vd, gates=gates)


def _front_prompt_kernel(x_ref, cos_ref, sin_ref, lng_ref, lnb_ref, win_ref, bg_ref, qn_ref, wuq_ref,
                         kvn_ref, wkv_ref,
                         lat_o, kr_o, kd_o, vd_o, qm_o, km_o, vm_o, qd_o, kdb_o, vdb_o, g_o):
    r = _front_core(x_ref[...], cos_ref[...], sin_ref[...], lng_ref[...], lnb_ref[...], win_ref,
                    bg_ref[...], qn_ref[...], wuq_ref, kvn_ref[...], wkv_ref)
    lat_o[0, 0] = r["lat"]
    kr_o[0, 0] = r["kr"][:, :ROPE]
    kd_o[0, 0] = r["kd"]
    vd_o[0, 0] = r["vd"]
    qm_o[...] = r["q"].astype(BF16)
    km_o[...] = r["k_mla"].astype(BF16)
    vm_o[...] = r["v_mla"].astype(BF16)
    qd_o[...] = r["qd"].astype(BF16)
    kdb_o[...] = r["kd"].astype(BF16)
    vdb_o[...] = r["vd"].astype(BF16)
    g_o[...] = r["gates"]


def _front_meta_kernel(x_ref, cos_ref, sin_ref, lng_ref, lnb_ref, win_ref, bg_ref, qn_ref, wuq_ref,
                       kvn_ref, wkv_ref, lat_in, kr_in, kd_in, vd_in,
                       lat_o, kr_o, kd_o, vd_o, km_o, vm_o, kdb_o, vdb_o):
    del lat_in, kr_in, kd_in, vd_in
    r = _front_core(x_ref[...], cos_ref[...], sin_ref[...], lng_ref[...], lnb_ref[...], win_ref,
                    bg_ref[...], qn_ref[...], wuq_ref, kvn_ref[...], wkv_ref)
    n = N_META_TOK
    lat_o[...] = r["lat"][:n]
    kr_o[...] = r["kr"][:n, :ROPE]
    kd_o[...] = r["kd"][:n]
    vd_o[...] = r["vd"][:n]
    km_o[...] = r["k_mla"].astype(BF16)
    vm_o[...] = r["v_mla"].astype(BF16)
    kdb_o[...] = r["kd"].astype(BF16)
    vdb_o[...] = r["vd"].astype(BF16)


def _front_sample_kernel(x_ref, cos_ref, sin_ref, lng_ref, lnb_ref, win_ref, bg_ref, qn_ref, wuq_ref,
                         kvn_ref, wkv_ref,
                         lat_o, kr_o, kd_o, vd_o, g_o, qlat_o, qrope_o, qd_o, ss_o):
    r = _front_core(x_ref[...], cos_ref[...], sin_ref[...], lng_ref[...], lnb_ref[...], win_ref,
                    bg_ref[...], qn_ref[...], wuq_ref, kvn_ref[...], wkv_ref)
    nb = x_ref.shape[0]
    lat, kr, kd, q, qd = r["lat"], r["kr"], r["kd"], r["q"], r["qd"]
    lat_o[...] = lat
    kr_o[...] = kr[:, :ROPE]
    kd_o[...] = kd
    vd_o[...] = r["vd"]
    g_o[...] = r["gates"]
    qd_o[...] = qd

    lane = lax.broadcasted_iota(jnp.int32, (nb, LANES), 1)
    kr_at_rope = pltpu.roll(kr, NOPE, 1)
    rope_lanes = (lane >= NOPE) & (lane < NOPE + ROPE)
    rep = lambda col: jnp.broadcast_to(col, (nb, LANES))
    for h in range(HEADS):
        q_slab = q[:, SLAB * h:SLAB * (h + 1)]
        q_lat = _dot_nt(q_slab.astype(BF16), wkv_ref[:, SLAB * h:SLAB * (h + 1)])
        qlat_o[h] = q_lat
        qrope_o[h] = pltpu.roll(q_slab, LANES - NOPE, 1)[:, :ROPE]
        s_h = (jnp.sum(q_lat * lat, axis=1, keepdims=True)
               + jnp.sum(jnp.where(rope_lanes, q_slab * kr_at_rope, 0.0), axis=1, keepdims=True))
        ss_o[h] = rep(s_h)
    lane4 = lax.broadcasted_iota(jnp.int32, (nb, 4 * LANES), 1)
    prod = qd * kd
    for h in range(HEADS):
        for m in range(2):
            lo = h * 2 * DIFF_DIM + m * DIFF_DIM
            seg = (lane4 >= lo) & (lane4 < lo + DIFF_DIM)
            ss_o[HEADS + HEADS * m + h] = rep(jnp.sum(jnp.where(seg, prod, 0.0), axis=1, keepdims=True))
    for c in range(3 * HEADS, DEC_ROWS):
        ss_o[c] = jnp.zeros((nb, LANES), F32)


def _stream_init(st, q, kc, vc, valid, m_scr, l_scr, acc_scr):
    s = jnp.where(valid, _dot_nt(q, kc), NEG_INF)
    m = jnp.max(s, axis=1, keepdims=True)
    p = jnp.exp(s - m)
    m_scr[st] = jnp.broadcast_to(m, m_scr.shape[1:])
    l_scr[st] = jnp.broadcast_to(jnp.sum(p, axis=1, keepdims=True), l_scr.shape[1:])
    acc_scr[st] = _dot(p.astype(BF16), vc)


def _stream_update(st, q, kc, vc, valid, m_scr, l_scr, acc_scr):
    s = _dot_nt(q, kc)
    if valid is not None:
        s = jnp.where(valid, s, NEG_INF)
    m_prev = m_scr[st]
    m_new = jnp.maximum(m_prev, jnp.max(s, axis=1, keepdims=True))
    corr = jnp.exp(m_prev - m_new)
    reps = s.shape[1] // LANES
    p = jnp.exp(s - (jnp.tile(m_new, (1, reps)) if reps > 1 else m_new))
    l_scr[st] = corr * l_scr[st] + jnp.sum(p, axis=1, keepdims=True)
    acc_scr[st] = corr * acc_scr[st] + _dot(p.astype(BF16), vc)
    m_scr[st] = m_new


def _causal_sweep(qi, streams, k_ref, v_ref, km_ref, vm_ref, m_scr, l_scr, acc_scr):
    wk = streams[0][0].shape[1]
    col = lax.broadcasted_iota(jnp.int32, (TM, LANES), 1)
    meta_valid = col < N_META_TOK
    vmeta = vm_ref[...]
    for st, (q, ko) in enumerate(streams):
        _stream_init(st, q, km_ref[:, ko:ko + wk], vmeta, meta_valid, m_scr, l_scr, acc_scr)

    def full_chunk(c, carry):
        start = pl.multiple_of(c * TK, TK)
        vc = v_ref[pl.ds(start, TK), :]
        for st, (q, ko) in enumerate(streams):
            _stream_update(st, q, k_ref[pl.ds(start, TK), ko:ko + wk], vc, None, m_scr, l_scr, acc_scr)
        return carry

    lax.fori_loop(0, qi // 2, full_chunk, 0)

    @pl.when(qi % 2 == 1)
    def _():
        start = pl.multiple_of((qi - 1) * TM, TM)
        vc = v_ref[pl.ds(start, TM), :]
        for st, (q, ko) in enumerate(streams):
            _stream_update(st, q, k_ref[pl.ds(start, TM), ko:ko + wk], vc, None, m_scr, l_scr, acc_scr)

    start = pl.multiple_of(qi * TM, TM)
    row = lax.broadcasted_iota(jnp.int32, (TM, TM), 0)
    colq = lax.broadcasted_iota(jnp.int32, (TM, TM), 1)
    diag_valid = colq <= row
    vc = v_ref[pl.ds(start, TM), :]
    for st, (q, ko) in enumerate(streams):
        _stream_update(st, q, k_ref[pl.ds(start, TM), ko:ko + wk], vc, diag_valid, m_scr, l_scr, acc_scr)


def _attn_mla_kernel(q_ref, k_ref, v_ref, km_ref, vm_ref, o_ref, m_scr, l_scr, acc_scr):
    qi = pl.program_id(2)
    streams = [(q_ref[:, SLAB * hh:SLAB * (hh + 1)], SLAB * hh) for hh in range(2)]
    _causal_sweep(qi, streams, k_ref, v_ref, km_ref, vm_ref, m_scr, l_scr, acc_scr)
    lane = lax.broadcasted_iota(jnp.int32, (TM, LANES), 1)
    o = jnp.where(lane < MLA_V, acc_scr[0] / l_scr[0], acc_scr[1] / l_scr[1])
    o_ref[...] = o.astype(o_ref.dtype)


def _attn_diff_kernel(q_ref, k_ref, v_ref, km_ref, vm_ref, dl_ref, g_ref, o_ref, m_scr, l_scr, acc_scr):
    qi = pl.program_id(2)
    q = q_ref[...]
    lane = lax.broadcasted_iota(jnp.int32, (TM, LANES), 1)
    zero = jnp.zeros_like(q)
    streams = []
    for hh in range(2):
        for m in range(2):
            lo = hh * 2 * DIFF_DIM + m * DIFF_DIM
            streams.append((jnp.where((lane >= lo) & (lane < lo + DIFF_DIM), q, zero), 0))
    _causal_sweep(qi, streams, k_ref, v_ref, km_ref, vm_ref, m_scr, l_scr, acc_scr)
    lam = _lambda(dl_ref[...])
    o0 = acc_scr[0] / l_scr[0] - lam * (acc_scr[1] / l_scr[1])
    o1 = acc_scr[2] / l_scr[2] - lam * (acc_scr[3] / l_scr[3])
    o = jnp.where(lane < DIFF_V, o0, o1)
    o_ref[...] = _sub_ln_slab(o, g_ref[...]).astype(o_ref.dtype)


def _decode_kernel(pt_ref, qlat_ref, qrope_ref, qd_ref, ss_ref, lats_ref, vds_ref, dl_ref, wuv_ref, *rest):
    del pt_ref
    g_n = DEC_G
    lat_refs, kr_refs = rest[0:g_n], rest[g_n:2 * g_n]
    kt_refs, vt_refs = rest[2 * g_n:3 * g_n], rest[3 * g_n:4 * g_n]
    om_ref, od_ref = rest[4 * g_n], rest[4 * g_n + 1]
    qcol, ml, ll, accl, md, ld, accd = rest[4 * g_n + 2:]
    b = pl.program_id(0)
    t = pl.program_id(1)
    n_t = pl.num_programs(1)
    r = b % SUBLANES
    n_maps = 2 * HEADS

    @pl.when(t == 0)
    def _():
        qrow = qd_ref[pl.ds(r, 1), :]
        for s in range(HEADS // 2):
            cols = jnp.broadcast_to(qrow[:, LANES * s:LANES * (s + 1)], (LANES, LANES)).T
            qcol[2 * s] = cols[0:2 * DIFF_DIM]
            qcol[2 * s + 1] = cols[2 * DIFF_DIM:]
        ss = ss_ref[...]
        ml[...] = ss[0:HEADS]
        md[...] = ss[HEADS:HEADS + n_maps]
        ll[...] = jnp.zeros(ll.shape, F32)
        ld[...] = jnp.zeros(ld.shape, F32)
        accl[...] = jnp.zeros(accl.shape, F32)
        accd[...] = jnp.zeros(accd.shape, F32)

    qlat = qlat_ref[...].astype(BF16)
    qrope = qrope_ref[...].astype(BF16)
    lat_b = [ref[...].astype(BF16) for ref in lat_refs]
    s_l = [_dot_nt(qlat, lat_b[g]) + _dot(qrope, kr_refs[g][...].astype(BF16)) for g in range(g_n)]
    mx = s_l[0]
    for g in range(1, g_n):
        mx = jnp.maximum(mx, s_l[g])
    m_old = ml[...]
    m_new = jnp.maximum(m_old, jnp.max(mx, axis=1, keepdims=True))
    corr = jnp.exp(m_old - m_new)
    psum = jnp.zeros((HEADS, LANES), F32)
    o_l = jnp.zeros((HEADS, KV_RANK), F32)
    for g in range(g_n):
        p = jnp.exp(s_l[g] - m_new)
        psum = psum + p
        o_l = o_l + _dot(p.astype(BF16), lat_b[g])
    ll[...] = ll[...] * corr + jnp.sum(psum, axis=1, keepdims=True)
    accl[...] = accl[...] * jnp.tile(corr, (1, KV_RANK // LANES)) + o_l
    ml[...] = m_new

    for h in range(HEADS):
        qc = qcol[h]
        rows = ([], [])
        for g in range(g_n):
            prod = kt_refs[g][h] * qc
            rows[0].append(jnp.sum(prod[0:DIFF_DIM], axis=0, keepdims=True))
            rows[1].append(jnp.sum(prod[DIFF_DIM:], axis=0, keepdims=True))
        for m in range(2):
            idx = HEADS * m + h
            s_d = jnp.concatenate(rows[m], axis=0)
            m_old = md[idx:idx + 1, :]
            m_new = jnp.maximum(
                m_old, jnp.max(jnp.max(s_d, axis=1, keepdims=True), axis=0, keepdims=True))
            corr = jnp.exp(m_old - m_new)
            p_d = jnp.exp(s_d - m_new)
            ld[idx:idx + 1, :] = ld[idx:idx + 1, :] * corr + jnp.sum(
                jnp.sum(p_d, axis=1, keepdims=True), axis=0, keepdims=True)
            acc = accd[idx] * corr
            for g in range(g_n):
                acc = acc + vt_refs[g][h] * p_d[g:g + 1, :]
            accd[idx] = acc
            md[idx:idx + 1, :] = m_new

    @pl.when(t == n_t - 1)
    def _():
        ss = ss_ref[...]
        w_self = jnp.exp(ss[0:HEADS] - ml[...])
        l_tot = ll[...] + w_self
        reps = KV_RANK // LANES
        o_lat = (accl[...] + jnp.tile(w_self, (1, reps)) * lats_ref[pl.ds(r, 1), :]) / jnp.tile(l_tot, (1, reps))
        om_full = _dot(o_lat.astype(BF16), wuv_ref[...])
        row8 = lax.broadcasted_iota(jnp.int32, (HEADS, MLA_V), 0)
        om = jnp.zeros((HEADS, MLA_V), F32)
        for h in range(HEADS):
            om = jnp.where(row8 == h, om_full[:, MLA_V * h:MLA_V * (h + 1)], om)
        om_ref[...] = om

        lane = lax.broadcasted_iota(jnp.int32, (DIFF_V, LANES), 1)
        cols = jnp.zeros((DIFF_V, LANES), F32)
        for idx in range(n_maps):
            cols = jnp.where(lane == idx, jnp.sum(accd[idx], axis=1, keepdims=True), cols)
        o_rows = jnp.concatenate([cols, jnp.zeros((LANES - DIFF_V, LANES), F32)], axis=0).T
        w_d = jnp.exp(ss[HEADS:HEADS + n_maps] - md[...])
        l_d = ld[...] + w_d
        vd_self = vds_ref[...]
        o1 = (o_rows[0:HEADS, 0:DIFF_V] + w_d[0:HEADS, 0:DIFF_V] * vd_self) / l_d[0:HEADS, 0:DIFF_V]
        o2 = (o_rows[HEADS:n_maps, 0:DIFF_V] + w_d[HEADS:, 0:DIFF_V] * vd_self) / l_d[HEADS:, 0:DIFF_V]
        od_ref[...] = o1 - _lambda(dl_ref[...]) * o2


def _back_kernel(raw_diff, x_ref, om_ref, od_ref, g_ref, lng_ref, lnb_ref, subln_ref, wbm_ref, wbd_ref,
                 wout_ref, ln1g_ref, ln1b_ref, wup_ref, wdn_ref, ln2g_ref, ln2b_ref, y_ref):
    h = _layer_norm(x_ref[...], lng_ref[...], lnb_ref[...])
    od = od_ref[...]
    if raw_diff:
        g128 = subln_ref[...]
        od = jnp.concatenate(
            [_sub_ln_slab(od[:, LANES * s:LANES * (s + 1)], g128) for s in range(od.shape[1] // LANES)], axis=1)
    a = _dot(om_ref[...].astype(BF16), wbm_ref[...])
    b = _dot(od.astype(BF16), wbd_ref[...])
    g = g_ref[...]
    mix = _dot((g[:, :D_MODEL] * a + g[:, D_MODEL:] * b).astype(BF16), wout_ref[...])
    x1 = _layer_norm(DEEPNORM_ALPHA * h + mix, ln1g_ref[...], ln1b_ref[...])
    up = jnp.maximum(_dot(x1.astype(BF16), wup_ref[...]), 0.0)
    f = _dot((up * up).astype(BF16), wdn_ref[...])
    y_ref[...] = _layer_norm(DEEPNORM_ALPHA * x1 + f, ln2g_ref[...], ln2b_ref[...])


def _const_spec(shape):
    nd = len(shape)
    return pl.BlockSpec(shape, lambda *_: (0,) * nd, pipeline_mode=pl.Buffered(1))


def _rope_tables(pos):
    inv = ROPE_THETA ** (-jnp.arange(HALF, dtype=F32) / HALF)
    ang = pos.astype(F32)[:, None] * inv[None, :]
    reps = LANES // HALF
    return jnp.tile(jnp.cos(ang), (1, reps)), jnp.tile(jnp.sin(ang), (1, reps))


def kernel(x_prompt, x_sample, cache_mla_latent, cache_mla_krope, cache_diff_k, cache_diff_v, page_table,
           meta_tokens, ln_in_g, ln_in_b, w_in, b_gate, mla_q_norm, w_uq, mla_kv_norm, w_uk, w_uv,
           diff_lambda, diff_subln, w_br_mla, w_br_diff, w_out, ln1_g, ln1_b, w_up, w_down, ln2_g, ln2_b):
    n_b, seq, d = x_prompt.shape
    n_dec = x_sample.shape[0]
    n_pages = page_table.shape[1]
    l_all = seq + N_META_TOK
    nq = seq // TM
    layer = 0

    w = w_in[layer]
    win = jnp.concatenate(
        [w[:, 0:512], w[:, 544:4128], w[:, 512:544], jnp.zeros((d, C_END - C_KR - ROPE), F32)],
        axis=1).astype(BF16)
    wuq = jnp.pad(w_uq[layer].reshape(Q_RANK, HEADS, NOPE + ROPE),
                  ((0, 0), (0, 0), (0, SLAB - NOPE - ROPE))).reshape(Q_RANK, HEADS * SLAB).astype(BF16)
    wuk = jnp.pad(w_uk[layer], ((0, 0), (0, 0), (0, SLAB - NOPE))).reshape(KV_RANK, HEADS * SLAB)
    wuv = w_uv[layer].reshape(KV_RANK, HEADS * MLA_V)
    wkv = jnp.concatenate([wuk, wuv], axis=1).astype(BF16)
    wuv_b = wuv.astype(BF16)
    row = lambda v: v.reshape(1, -1)
    lng, lnb = row(ln_in_g), row(ln_in_b)
    bg, qn, kvn = row(b_gate[layer]), row(mla_q_norm[layer]), row(mla_kv_norm[layer])
    subln = row(jnp.tile(diff_subln[layer], LANES // DIFF_V))
    dl = diff_lambda[layer].astype(F32)
    front_params = (lng, lnb, win, bg, qn, wuq, kvn, wkv)
    front_param_specs = [_const_spec(p.shape) for p in front_params]

    cos_p, sin_p = _rope_tables(jnp.arange(N_META_TOK, l_all))
    cos_m, sin_m = _rope_tables(jnp.arange(LANES))
    cos_s, sin_s = _rope_tables(jnp.full((n_dec,), n_pages * PAGE))

    tok = lambda wdt, dt: jax.ShapeDtypeStruct((n_b, seq, wdt), dt)
    cache_shape = lambda wdt: jax.ShapeDtypeStruct((DEPTH, n_b, l_all, wdt), F32)
    tok_spec = lambda wdt: pl.BlockSpec((None, TM, wdt), lambda b, i: (b, i, 0))
    cache_spec = lambda wdt: pl.BlockSpec((pl.Element(1), pl.Element(1), pl.Element(TM), pl.Element(wdt)),
                                          lambda b, i: (0, b, pl.multiple_of(N_META_TOK + i * TM, N_META_TOK), 0))
    (lat_p, kr_p, kd_p, vd_p, q_mla, k_mla, v_mla, q_diff, k_diff, v_diff, gates_p) = pl.pallas_call(
        _front_prompt_kernel,
        grid=(n_b, nq),
        in_specs=[tok_spec(d), pl.BlockSpec((TM, LANES), lambda b, i: (i, 0)),
                  pl.BlockSpec((TM, LANES), lambda b, i: (i, 0))] + front_param_specs,
        out_specs=[cache_spec(KV_RANK), cache_spec(ROPE), cache_spec(512), cache_spec(512),
                   tok_spec(HEADS * SLAB), tok_spec(HEADS * SLAB), tok_spec(512),
                   tok_spec(512), tok_spec(512), tok_spec(512), tok_spec(2 * d)],
        out_shape=[cache_shape(KV_RANK), cache_shape(ROPE), cache_shape(512), cache_shape(512),
                   tok(HEADS * SLAB, BF16), tok(HEADS * SLAB, BF16), tok(512, BF16),
                   tok(512, BF16), tok(512, BF16), tok(512, BF16), tok(2 * d, F32)],
        compiler_params=_cparams(("arbitrary", "arbitrary"), 48),
        name="front_prompt",
    )(x_prompt, cos_p, sin_p, *front_params)

    meta_pad = jnp.pad(meta_tokens.astype(F32), ((0, LANES - N_META_TOK), (0, 0)))
    any_spec = pl.BlockSpec(memory_space=pl.ANY)
    meta_cache_spec = lambda wdt: pl.BlockSpec((None, None, N_META_TOK, wdt), lambda b: (0, b, 0, 0))
    mrow = lambda wdt: jax.ShapeDtypeStruct((LANES, wdt), BF16)
    mrow_spec = lambda wdt: pl.BlockSpec((LANES, wdt), lambda b: (0, 0))
    n_fp = 3 + len(front_params)
    (lat_p, kr_p, kd_p, vd_p, km_meta, vm_meta, kd_meta, vd_meta) = pl.pallas_call(
        _front_meta_kernel,
        grid=(n_b,),
        in_specs=[_const_spec((LANES, d)), _const_spec((LANES, LANES)), _const_spec((LANES, LANES))]
        + front_param_specs + [any_spec] * 4,
        out_specs=[meta_cache_spec(KV_RANK), meta_cache_spec(ROPE), meta_cache_spec(512), meta_cache_spec(512),
                   mrow_spec(HEADS * SLAB), mrow_spec(512), mrow_spec(512), mrow_spec(512)],
        out_shape=[cache_shape(KV_RANK), cache_shape(ROPE), cache_shape(512), cache_shape(512),
                   mrow(HEADS * SLAB), mrow(512), mrow(512), mrow(512)],
        input_output_aliases={n_fp: 0, n_fp + 1: 1, n_fp + 2: 2, n_fp + 3: 3},
        compiler_params=_cparams(("arbitrary",), 40),
        name="front_meta",
    )(meta_pad, cos_m, sin_m, *front_params, lat_p, kr_p, kd_p, vd_p)

    xs = x_sample.reshape(n_dec, d)
    sds = lambda shape, dt: jax.ShapeDtypeStruct(shape, dt)
    full = lambda shape: pl.BlockSpec(shape, lambda i: (0,) * len(shape))
    s_shapes = [(n_dec, KV_RANK), (n_dec, ROPE), (n_dec, 512), (n_dec, 512), (n_dec, 2 * d),
                (HEADS, n_dec, KV_RANK), (HEADS, n_dec, ROPE), (n_dec, 512), (DEC_ROWS, n_dec, LANES)]
    (lat_s, kr_s, kd_s, vd_s, gates_s, qlat_s, qrope_s, qd_s, ss_s) = pl.pallas_call(
        _front_sample_kernel,
        grid=(1,),
        in_specs=[_const_spec((n_dec, d)), _const_spec((n_dec, LANES)), _const_spec((n_dec, LANES))]
        + front_param_specs,
        out_specs=[full(s) for s in s_shapes],
        out_shape=[sds(s, F32) for s in s_shapes],
        compiler_params=_cparams(("arbitrary",), 40),
        name="front_sample",
    )(xs, cos_s, sin_s, *front_params)

    n_pair = HEADS // 2
    attn_grid = (n_b, n_pair, nq)
    attn_sem = ("arbitrary", "arbitrary", "arbitrary")
    q_spec = lambda wdt: pl.BlockSpec((None, TM, wdt), lambda b, p, i: (b, i, p))
    kv_spec = lambda wdt: pl.BlockSpec((None, seq, wdt), lambda b, p, i: (b, 0, p))
    meta_spec = lambda wdt: pl.BlockSpec((LANES, wdt), lambda b, p, i: (0, p))
    o_spec = pl.BlockSpec((None, TM, LANES), lambda b, p, i: (b, i, p))
    o_shape = jax.ShapeDtypeStruct((n_b, seq, HEADS * MLA_V), BF16)

    def attn_scratch(n_streams):
        return [pltpu.VMEM((n_streams, TM, LANES), F32) for _ in range(3)]

    o_mla = pl.pallas_call(
        _attn_mla_kernel,
        grid=attn_grid,
        in_specs=[q_spec(2 * SLAB), kv_spec(2 * SLAB), kv_spec(LANES), meta_spec(2 * SLAB), meta_spec(LANES)],
        out_specs=o_spec,
        out_shape=o_shape,
        scratch_shapes=attn_scratch(2),
        compiler_params=_cparams(attn_sem, 40),
        name="attn_mla",
    )(q_mla, k_mla, v_mla, km_meta, vm_meta)

    o_diff = pl.pallas_call(
        _attn_diff_kernel,
        grid=attn_grid,
        in_specs=[q_spec(LANES), kv_spec(LANES), kv_spec(LANES), meta_spec(LANES), meta_spec(LANES),
                  pl.BlockSpec(dl.shape, lambda b, p, i: (0, 0)), pl.BlockSpec(subln.shape, lambda b, p, i: (0, 0))],
        out_specs=o_spec,
        out_shape=o_shape,
        scratch_shapes=attn_scratch(4),
        compiler_params=_cparams(attn_sem, 40),
        name="attn_diff",
    )(q_diff, k_diff, v_diff, kd_meta, vd_meta, dl, subln)

    n_t = n_pages // DEC_G
    pt_flat = page_table.reshape(-1).astype(jnp.int32)
    kr_t = jnp.transpose(cache_mla_krope, (0, 1, 3, 2))
    kd_t = jnp.transpose(cache_diff_k, (0, 1, 3, 4, 2))
    vd_t = jnp.transpose(cache_diff_v, (0, 1, 3, 4, 2))
    page_idx = lambda b, t, pt, g: pt[b * n_pages + t * DEC_G + g]
    lat_spec = lambda g: pl.BlockSpec((None, None, PAGE, KV_RANK),
                                      lambda b, t, pt: (layer, page_idx(b, t, pt, g), 0, 0))
    krt_spec = lambda g: pl.BlockSpec((None, None, ROPE, PAGE),
                                      lambda b, t, pt: (layer, page_idx(b, t, pt, g), 0, 0))
    hdt_spec = lambda g: pl.BlockSpec((None, None, HEADS, 2 * DIFF_DIM, PAGE),
                                      lambda b, t, pt: (layer, page_idx(b, t, pt, g), 0, 0, 0))
    qlat_b = jnp.transpose(qlat_s, (1, 0, 2))
    qrope_b = jnp.transpose(qrope_s, (1, 0, 2))
    ss_b = jnp.transpose(ss_s, (1, 0, 2))
    per_seq = lambda rows, wdt: pl.BlockSpec((None, rows, wdt), lambda b, t, pt: (b, 0, 0))
    group8 = lambda wdt: pl.BlockSpec((SUBLANES, wdt), lambda b, t, pt: (b // SUBLANES, 0))
    dec_in_specs = (
        [per_seq(HEADS, KV_RANK), per_seq(HEADS, ROPE), group8(512), per_seq(DEC_ROWS, LANES),
         group8(KV_RANK), per_seq(HEADS, DIFF_V),
         pl.BlockSpec(dl.shape, lambda b, t, pt: (0, 0)), pl.BlockSpec(wuv_b.shape, lambda b, t, pt: (0, 0))]
        + [lat_spec(g) for g in range(DEC_G)] + [krt_spec(g) for g in range(DEC_G)]
        + [hdt_spec(g) for g in range(DEC_G)] + [hdt_spec(g) for g in range(DEC_G)])
    dec_out_spec = pl.BlockSpec((None, HEADS, MLA_V), lambda b, t, pt: (b, 0, 0))
    om_s, od_s = pl.pallas_call(
        _decode_kernel,
        grid_spec=pltpu.PrefetchScalarGridSpec(
            num_scalar_prefetch=1,
            grid=(n_dec, n_t),
            in_specs=dec_in_specs,
            out_specs=[dec_out_spec, dec_out_spec],
            scratch_shapes=[pltpu.VMEM((HEADS, 2 * DIFF_DIM, LANES), F32),
                            pltpu.VMEM((HEADS, LANES), F32), pltpu.VMEM((HEADS, LANES), F32),
                            pltpu.VMEM((HEADS, KV_RANK), F32),
                            pltpu.VMEM((2 * HEADS, LANES), F32), pltpu.VMEM((2 * HEADS, LANES), F32),
                            pltpu.VMEM((2 * HEADS, DIFF_V, LANES), F32)]),
        out_shape=[jax.ShapeDtypeStruct((n_dec, HEADS, MLA_V), F32)] * 2,
        compiler_params=_cparams(("arbitrary", "arbitrary"), 48),
        name="decode_attn",
    )(pt_flat, qlat_b, qrope_b, qd_s, ss_b, lat_s, vd_s.reshape(n_dec, HEADS, DIFF_V), dl, wuv_b,
      *([cache_mla_latent] * DEC_G), *([kr_t] * DEC_G), *([kd_t] * DEC_G), *([vd_t] * DEC_G))

    back_params = (lng, lnb, subln, w_br_mla[layer].astype(BF16), w_br_diff[layer].astype(BF16),
                   w_out[layer].astype(BF16), row(ln1_g[layer]), row(ln1_b[layer]), w_up[layer].astype(BF16),
                   w_down[layer].astype(BF16), row(ln2_g[layer]), row(ln2_b[layer]))
    back_param_specs = [_const_spec(p.shape) for p in back_params]
    y_prompt = pl.pallas_call(
        functools.partial(_back_kernel, False),
        grid=(n_b, nq),
        in_specs=[tok_spec(d), tok_spec(512), tok_spec(512), tok_spec(2 * d)] + back_param_specs,
        out_specs=tok_spec(d),
        out_shape=jax.ShapeDtypeStruct((n_b, seq, d), F32),
        compiler_params=_cparams(("arbitrary", "arbitrary"), 56),
        name="back_prompt",
    )(x_prompt, o_mla, o_diff, gates_p, *back_params)

    y_sample = pl.pallas_call(
        functools.partial(_back_kernel, True),
        grid=(1,),
        in_specs=[full((n_dec, d)), full((n_dec, 512)), full((n_dec, 512)), full((n_dec, 2 * d))]
        + back_param_specs,
        out_specs=full((n_dec, d)),
        out_shape=jax.ShapeDtypeStruct((n_dec, d), F32),
        compiler_params=_cparams(("arbitrary",), 56),
        name="back_sample",
    )(xs, om_s.reshape(n_dec, 512), od_s.reshape(n_dec, 512), gates_s, *back_params)

    head_shape = lambda a, lead, v: a.reshape(lead + (HEADS, v))
    return (y_prompt, y_sample.reshape(n_dec, 1, d),
            lat_p, kr_p,
            head_shape(kd_p, (DEPTH, n_b, l_all), 2 * DIFF_DIM), head_shape(vd_p, (DEPTH, n_b, l_all), DIFF_V),
            lat_s.reshape(DEPTH, n_dec, 1, KV_RANK), kr_s.reshape(DEPTH, n_dec, 1, ROPE),
            head_shape(kd_s, (DEPTH, n_dec, 1), 2 * DIFF_DIM), head_shape(vd_s, (DEPTH, n_dec, 1), DIFF_V))
```

```python
import functools
import math

import jax
import jax.numpy as jnp
from jax import lax
from jax.experimental import pallas as pl
from jax.experimental.pallas import tpu as pltpu

F32 = jnp.float32
BF16 = jnp.bfloat16

D_MODEL = 1024
N_META_TOK = 16
PAGE = 128
HEADS = 8
Q_RANK = 256
KV_RANK = 256
NOPE = 64
ROPE = 32
MLA_V = 64
DIFF_DIM = 32
DIFF_V = 64
D_FF = 4 * D_MODEL
DEPTH = 1
ROPE_THETA = 10000.0
LN_EPS = 1e-5
RMS_EPS = 1e-6
LOG2E = math.log2(math.e)
MLA_SCALE = (NOPE + ROPE) ** -0.5 * LOG2E
DIFF_SCALE = DIFF_DIM ** -0.5 * LOG2E
DEEPNORM_ALPHA = (2 * DEPTH) ** 0.25
LAM_INIT = 0.8 - 0.6 * math.exp(-0.3 * 0)

LANES = 128
SUBLANES = 8
SLAB = 128
HALF = DIFF_DIM // 2

C_CQ, C_CKV, C_DQ, C_DK, C_DV, C_G, C_KR, C_END = 0, 256, 512, 1024, 1536, 2048, 4096, 4224

TM = 256
TQ = 512
TK = 512
DEC_G = 8
DEC_BUFS = 3
DEC_ROWS = 32
NEG_INF = float("-inf")


def _cparams(sem, vmem_mb):
    return pltpu.CompilerParams(dimension_semantics=sem, vmem_limit_bytes=vmem_mb * 1024 * 1024)


def _dot(a, b):
    return jnp.dot(a, b, preferred_element_type=F32)


def _dot_nt(a, b):
    return lax.dot_general(a, b, (((1,), (1,)), ((), ())), preferred_element_type=F32)


def _layer_norm(x, g, b):
    mu = jnp.mean(x, axis=-1, keepdims=True)
    xc = x - mu
    var = jnp.mean(xc * xc, axis=-1, keepdims=True)
    return xc * lax.rsqrt(var + LN_EPS) * g + b


def _rms_norm(x, g):
    return x * lax.rsqrt(jnp.mean(x * x, axis=-1, keepdims=True) + RMS_EPS) * g


def _rope_slabs(x, c, s_up, s_dn):
    outs = []
    for s in range(x.shape[1] // LANES):
        xs = x[:, LANES * s:LANES * (s + 1)]
        outs.append(xs * c + pltpu.roll(xs, HALF, 1) * s_up + pltpu.roll(xs, LANES - HALF, 1) * s_dn)
    return outs[0] if len(outs) == 1 else jnp.concatenate(outs, axis=1)


def _sub_ln_slab(x, g128):
    lane = lax.broadcasted_iota(jnp.int32, x.shape, 1)
    lo = lane < DIFF_V
    x2 = x * x
    s0 = jnp.sum(jnp.where(lo, x2, 0.0), axis=1, keepdims=True)
    s1 = jnp.sum(jnp.where(lo, 0.0, x2), axis=1, keepdims=True)
    ms = jnp.where(lo, s0, s1) * (1.0 / DIFF_V)
    return x * lax.rsqrt(ms + RMS_EPS) * g128 * (1.0 - LAM_INIT)


def _lambda(dl):
    a = jnp.sum(dl[0:1, :] * dl[1:2, :], axis=1, keepdims=True)
    b = jnp.sum(dl[2:3, :] * dl[3:4, :], axis=1, keepdims=True)
    return jnp.exp(a) - jnp.exp(b) + LAM_INIT


def _front_core(x, cosb, sinb, lng, lnb, win_ref, bg, qn, wuq_ref, kvn, wkv_ref):
    tm = x.shape[0]
    hb = _layer_norm(x, lng, lnb).astype(BF16)

    lane = lax.broadcasted_iota(jnp.int32, (tm, LANES), 1)
    first = (lane & (DIFF_DIM - 1)) < HALF
    zero = jnp.zeros_like(sinb)
    s_up_d = jnp.where(first, zero, sinb)
    s_dn_d = jnp.where(first, -sinb, zero)
    c_q = jnp.where(lane < NOPE, 1.0, cosb)
    s_up_q = jnp.where((lane >= NOPE + HALF) & (lane < NOPE + ROPE), sinb, zero)
    s_dn_q = jnp.where((lane >= NOPE) & (lane < NOPE + HALF), -sinb, zero)
    c_k = jnp.where(lane < ROPE, cosb, zero)
    s_up_k = jnp.where((lane >= HALF) & (lane < ROPE), sinb, zero)
    s_dn_k = jnp.where(lane < HALF, -sinb, zero)

    ya = _dot(hb, win_ref[:, C_CQ:C_DQ])
    cqn = _rms_norm(ya[:, :Q_RANK], qn).astype(BF16)
    q = _rope_slabs(_dot(cqn, wuq_ref[...]), c_q, s_up_q, s_dn_q) * MLA_SCALE
    lat = _rms_norm(ya[:, Q_RANK:], kvn)
    kv = _dot(lat.astype(BF16), wkv_ref[...])
    ykr = _dot(hb, win_ref[:, C_KR:C_END])
    kr = ykr * c_k + pltpu.roll(ykr, HALF, 1) * s_up_k + pltpu.roll(ykr, LANES - HALF, 1) * s_dn_k
    kr_at_rope = pltpu.roll(kr, NOPE, 1)
    k_mla = jnp.concatenate(
        [kv[:, SLAB * s:SLAB * (s + 1)] + kr_at_rope for s in range(HEADS)], axis=1)
    v_mla = kv[:, HEADS * SLAB:]

    yd = _dot(hb, win_ref[:, C_DQ:C_G])
    qd = _rope_slabs(yd[:, 0:512], cosb, s_up_d, s_dn_d) * DIFF_SCALE
    kd = _rope_slabs(yd[:, 512:1024], cosb, s_up_d, s_dn_d)
    vd = yd[:, 1024:1536]
    gates = jax.nn.sigmoid(_dot(hb, win_ref[:, C_G:C_KR]) + bg)
    return dict(q=q, lat=lat, kr=kr, k_mla=k_mla, v_mla=v_mla, qd=qd, kd=kd, vd=vd, gates=gates)


def _front_prompt_kernel(x_ref, cos_ref, sin_ref, lng_ref, lnb_ref, win_ref, bg_ref, qn_ref, wuq_ref,
                         kvn_ref, wkv_ref,
                         lat_o, kr_o, kd_o, vd_o, qm_o, km_o, vm_o, qd_o, kdb_o, vdb_o, g_o):
    r = _front_core(x_ref[...], cos_ref[...], sin_ref[...], lng_ref[...], lnb_ref[...], win_ref,
                    bg_ref[...], qn_ref[...], wuq_ref, kvn_ref[...], wkv_ref)
    lat_o[0, 0] = r["lat"]
    kr_o[0, 0] = r["kr"][:, :ROPE]
    kd_o[0, 0] = r["kd"]
    vd_o[0, 0] = r["vd"]
    qm_o[...] = r["q"].astype(BF16)
    km_o[...] = r["k_mla"].astype(BF16)
    vm_o[...] = r["v_mla"].astype(BF16)
    qd_o[...] = r["qd"].astype(BF16)
    kdb_o[...] = r["kd"].astype(BF16)
    vdb_o[...] = r["vd"].astype(BF16)
    g_o[...] = r["gates"]


def _front_meta_kernel(x_ref, cos_ref, sin_ref, lng_ref, lnb_ref, win_ref, bg_ref, qn_ref, wuq_ref,
                       kvn_ref, wkv_ref, lat_in, kr_in, kd_in, vd_in,
                       lat_o, kr_o, kd_o, vd_o, km_o, vm_o, kdb_o, vdb_o):
    del lat_in, kr_in, kd_in, vd_in
    r = _front_core(x_ref[...], cos_ref[...], sin_ref[...], lng_ref[...], lnb_ref[...], win_ref,
                    bg_ref[...], qn_ref[...], wuq_ref, kvn_ref[...], wkv_ref)
    n = N_META_TOK
    lat_o[...] = r["lat"][:n]
    kr_o[...] = r["kr"][:n, :ROPE]
    kd_o[...] = r["kd"][:n]
    vd_o[...] = r["vd"][:n]
    km_o[...] = r["k_mla"].astype(BF16)
    vm_o[...] = r["v_mla"].astype(BF16)
    kdb_o[...] = r["kd"].astype(BF16)
    vdb_o[...] = r["vd"].astype(BF16)


def _front_sample_kernel(x_ref, cos_ref, sin_ref, lng_ref, lnb_ref, win_ref, bg_ref, qn_ref, wuq_ref,
                         kvn_ref, wkv_ref,
                         lat_o, kr_o, kd_o, vd_o, g_o, qlat_o, qrope_o, qd_o, ss_o):
    r = _front_core(x_ref[...], cos_ref[...], sin_ref[...], lng_ref[...], lnb_ref[...], win_ref,
                    bg_ref[...], qn_ref[...], wuq_ref, kvn_ref[...], wkv_ref)
    nb = x_ref.shape[0]
    lat, kr, kd, q, qd = r["lat"], r["kr"], r["kd"], r["q"], r["qd"]
    lat_o[...] = lat
    kr_o[...] = kr[:, :ROPE]
    kd_o[...] = kd
    vd_o[...] = r["vd"]
    g_o[...] = r["gates"]
    qd_o[...] = qd

    lane = lax.broadcasted_iota(jnp.int32, (nb, LANES), 1)
    kr_at_rope = pltpu.roll(kr, NOPE, 1)
    rope_lanes = (lane >= NOPE) & (lane < NOPE + ROPE)
    rep = lambda col: jnp.broadcast_to(col, (nb, LANES))
    for h in range(HEADS):
        q_slab = q[:, SLAB * h:SLAB * (h + 1)]
        q_lat = _dot_nt(q_slab.astype(BF16), wkv_ref[:, SLAB * h:SLAB * (h + 1)])
        qlat_o[h] = q_lat
        qrope_o[h] = pltpu.roll(q_slab, LANES - NOPE, 1)[:, :ROPE]
        s_h = (jnp.sum(q_lat * lat, axis=1, keepdims=True)
               + jnp.sum(jnp.where(rope_lanes, q_slab * kr_at_rope, 0.0), axis=1, keepdims=True))
        ss_o[h] = rep(s_h)
    lane4 = lax.broadcasted_iota(jnp.int32, (nb, 4 * LANES), 1)
    prod = qd * kd
    for h in range(HEADS):
        for m in range(2):
            lo = h * 2 * DIFF_DIM + m * DIFF_DIM
            seg = (lane4 >= lo) & (lane4 < lo + DIFF_DIM)
            ss_o[HEADS + HEADS * m + h] = rep(jnp.sum(jnp.where(seg, prod, 0.0), axis=1, keepdims=True))
    for c in range(3 * HEADS, DEC_ROWS):
        ss_o[c] = jnp.zeros((nb, LANES), F32)


def _softmax_step(blocks, first, m_scr, l_scr, acc_scr):
    m_cur = None
    for s, _ in blocks:
        m_b = jnp.max(s, axis=1, keepdims=True)
        m_cur = m_b if m_cur is None else jnp.maximum(m_cur, m_b)
    if first:
        m_new = jnp.broadcast_to(m_cur, m_scr.shape)
    else:
        m_prev = m_scr[...]
        m_new = jnp.maximum(m_prev, m_cur)
        corr = jnp.exp2(m_prev - m_new)
    l_cur = None
    pv = None
    for s, v in blocks:
        reps = s.shape[1] // LANES
        p = jnp.exp2(s - (jnp.tile(m_new, (1, reps)) if reps > 1 else m_new))
        l_b = jnp.sum(p, axis=1, keepdims=True)
        pv_b = _dot(p.astype(BF16), v)
        l_cur = l_b if l_cur is None else l_cur + l_b
        pv = pv_b if pv is None else pv + pv_b
    if first:
        l_scr[...] = jnp.broadcast_to(l_cur, l_scr.shape)
        acc_scr[...] = pv
    else:
        l_scr[...] = corr * l_scr[...] + l_cur
        acc_scr[...] = corr * acc_scr[...] + pv
    m_scr[...] = m_new


def _causal_sweep(qi, groups, k_ref, v_ref, km_ref, vm_ref, m_scr, l_scr, acc_scr):
    def scores(k_at):
        parts = [_dot_nt(q, k_at(ko)) for q, ko in groups]
        return parts[0] if len(parts) == 1 else jnp.concatenate(parts, axis=0)

    rows = sum(q.shape[0] for q, _ in groups)
    ratio = TK // TQ

    def edge_blocks(n_rem):
        col = lax.broadcasted_iota(jnp.int32, (rows, LANES), 1)
        blocks = [(jnp.where(col < N_META_TOK, scores(lambda ko: km_ref[:, ko:ko + LANES]), NEG_INF), vm_ref[...])]
        for j in range(n_rem):
            start = pl.multiple_of((qi - n_rem + j) * TQ, TQ)
            blocks.append((scores(lambda ko: k_ref[pl.ds(start, TQ), ko:ko + LANES]), v_ref[pl.ds(start, TQ), :]))
        start = pl.multiple_of(qi * TQ, TQ)
        row = lax.broadcasted_iota(jnp.int32, (rows, TQ), 0)
        colq = lax.broadcasted_iota(jnp.int32, (rows, TQ), 1)
        s = scores(lambda ko: k_ref[pl.ds(start, TQ), ko:ko + LANES])
        blocks.append((jnp.where(colq <= (row & (TQ - 1)), s, NEG_INF), v_ref[pl.ds(start, TQ), :]))
        return blocks

    if ratio == 1:
        _softmax_step(edge_blocks(0), True, m_scr, l_scr, acc_scr)
    else:
        for n_rem in range(ratio):
            @pl.when(qi % ratio == n_rem)
            def _():
                _softmax_step(edge_blocks(n_rem), True, m_scr, l_scr, acc_scr)

    def full_chunk(c, carry):
        start = pl.multiple_of(c * TK, TK)
        s = scores(lambda ko: k_ref[pl.ds(start, TK), ko:ko + LANES])
        _softmax_step([(s, v_ref[pl.ds(start, TK), :])], False, m_scr, l_scr, acc_scr)
        return carry

    lax.fori_loop(0, qi // ratio, full_chunk, 0)


def _attn_mla_kernel(q_ref, k_ref, v_ref, km_ref, vm_ref, o_ref, m_scr, l_scr, acc_scr):
    qi = pl.program_id(2)
    groups = [(q_ref[:, SLAB * hh:SLAB * (hh + 1)], SLAB * hh) for hh in range(2)]
    _causal_sweep(qi, groups, k_ref, v_ref, km_ref, vm_ref, m_scr, l_scr, acc_scr)
    lane = lax.broadcasted_iota(jnp.int32, (TQ, LANES), 1)
    o = acc_scr[...] / l_scr[...]
    o_ref[...] = jnp.where(lane < MLA_V, o[0:TQ], o[TQ:2 * TQ]).astype(o_ref.dtype)


def _attn_diff_kernel(q_ref, k_ref, v_ref, km_ref, vm_ref, dl_ref, g_ref, o_ref, qs_scr, m_scr, l_scr, acc_scr):
    qi = pl.program_id(2)
    q = q_ref[...]
    lane = lax.broadcasted_iota(jnp.int32, (TQ, LANES), 1)
    zero = jnp.zeros_like(q)
    for j in range(4):
        qs_scr[j * TQ:(j + 1) * TQ, :] = jnp.where((lane >= DIFF_DIM * j) & (lane < DIFF_DIM * (j + 1)), q, zero)
    _causal_sweep(qi, [(qs_scr[...], 0)], k_ref, v_ref, km_ref, vm_ref, m_scr, l_scr, acc_scr)
    lam = _lambda(dl_ref[...])
    o = acc_scr[...] / l_scr[...]
    o0 = o[0:TQ] - lam * o[TQ:2 * TQ]
    o1 = o[2 * TQ:3 * TQ] - lam * o[3 * TQ:4 * TQ]
    o_ref[...] = _sub_ln_slab(jnp.where(lane < DIFF_V, o0, o1), g_ref[...]).astype(o_ref.dtype)


def _decode_kernel(n_steps, n_t, pt_ref, qlat_ref, qrope_ref, qd_ref, ss_ref, lats_ref, vds_ref, dl_ref, wuv_ref,
                   lat_hbm, kr_hbm, kt_hbm, vt_hbm, om_ref, od_ref,
                   lat_buf, kr_buf, kt_buf, vt_buf, sem, qcol, ml, ll, accl, md, ld, accd):
    g_n = DEC_G
    b = pl.program_id(0)
    t = pl.program_id(1)
    step = b * n_t + t
    r = b % SUBLANES
    n_maps = 2 * HEADS

    def page_copies(src_step, slot_, g):
        page = pt_ref[src_step * g_n + g]
        return (pltpu.make_async_copy(lat_hbm.at[0, page], lat_buf.at[slot_, g], sem.at[slot_, 0]),
                pltpu.make_async_copy(kr_hbm.at[0, page], kr_buf.at[slot_, g], sem.at[slot_, 1]),
                pltpu.make_async_copy(kt_hbm.at[0, page], kt_buf.at[slot_, g], sem.at[slot_, 2]),
                pltpu.make_async_copy(vt_hbm.at[0, page], vt_buf.at[slot_, g], sem.at[slot_, 3]))

    ahead = DEC_BUFS - 1
    slot = lax.rem(step, DEC_BUFS)
    nxt_step = lax.rem(step + ahead, n_steps)
    nxt_slot = lax.rem(step + ahead, DEC_BUFS)

    @pl.when(step == 0)
    def _():
        for s0 in range(ahead):
            for g in range(g_n):
                for c in page_copies(s0, s0, g):
                    c.start()

    for g in range(g_n):
        for c in page_copies(step, slot, g):
            c.wait()
    for g in range(g_n):
        for c in page_copies(nxt_step, nxt_slot, g):
            c.start()

    lat_refs = [lat_buf.at[slot, g] for g in range(g_n)]
    kr_refs = [kr_buf.at[slot, g] for g in range(g_n)]
    kt_refs = [kt_buf.at[slot, g] for g in range(g_n)]
    vt_refs = [vt_buf.at[slot, g] for g in range(g_n)]

    @pl.when(t == 0)
    def _():
        qrow = qd_ref[pl.ds(r, 1), :]
        for s in range(HEADS // 2):
            cols = jnp.broadcast_to(qrow[:, LANES * s:LANES * (s + 1)], (LANES, LANES)).T
            qcol[2 * s] = cols[0:2 * DIFF_DIM]
            qcol[2 * s + 1] = cols[2 * DIFF_DIM:]
        ss = ss_ref[...]
        ml[...] = ss[0:HEADS]
        md[...] = ss[HEADS:HEADS + n_maps]
        ll[...] = jnp.zeros(ll.shape, F32)
        ld[...] = jnp.zeros(ld.shape, F32)
        accl[...] = jnp.zeros(accl.shape, F32)
        accd[...] = jnp.zeros(accd.shape, F32)

    qlat = qlat_ref[...].astype(BF16)
    qrope = qrope_ref[...].astype(BF16)
    lat_b = [ref[...].astype(BF16) for ref in lat_refs]
    s_l = [_dot_nt(qlat, lat_b[g]) + _dot(qrope, kr_refs[g][...].astype(BF16)) for g in range(g_n)]
    mx = s_l[0]
    for g in range(1, g_n):
        mx = jnp.maximum(mx, s_l[g])
    m_old = ml[...]
    m_new = jnp.maximum(m_old, jnp.max(mx, axis=1, keepdims=True))
    corr = jnp.exp2(m_old - m_new)
    psum = jnp.zeros((HEADS, LANES), F32)
    o_l = jnp.zeros((HEADS, KV_RANK), F32)
    for g in range(g_n):
        p = jnp.exp2(s_l[g] - m_new)
        psum = psum + p
        o_l = o_l + _dot(p.astype(BF16), lat_b[g])
    ll[...] = ll[...] * corr + jnp.sum(psum, axis=1, keepdims=True)
    accl[...] = accl[...] * jnp.tile(corr, (1, KV_RANK // LANES)) + o_l
    ml[...] = m_new

    for h in range(HEADS):
        qc = qcol[h]
        rows = ([], [])
        for g in range(g_n):
            prod = kt_refs[g][h] * qc
            rows[0].append(jnp.sum(prod[0:DIFF_DIM], axis=0, keepdims=True))
            rows[1].append(jnp.sum(prod[DIFF_DIM:], axis=0, keepdims=True))
        for m in range(2):
            idx = HEADS * m + h
            s_d = jnp.concatenate(rows[m], axis=0)
            m_old = md[idx:idx + 1, :]
            m_new = jnp.maximum(
                m_old, jnp.max(jnp.max(s_d, axis=1, keepdims=True), axis=0, keepdims=True))
            corr = jnp.exp2(m_old - m_new)
            p_d = jnp.exp2(s_d - m_new)
            ld[idx:idx + 1, :] = ld[idx:idx + 1, :] * corr + jnp.sum(
                jnp.sum(p_d, axis=1, keepdims=True), axis=0, keepdims=True)
            acc = accd[idx] * corr
            for g in range(g_n):
                acc = acc + vt_refs[g][h] * p_d[g:g + 1, :]
            accd[idx] = acc
            md[idx:idx + 1, :] = m_new

    @pl.when(t == n_t - 1)
    def _():
        ss = ss_ref[...]
        w_self = jnp.exp2(ss[0:HEADS] - ml[...])
        l_tot = ll[...] + w_self
        reps = KV_RANK // LANES
        o_lat = (accl[...] + jnp.tile(w_self, (1, reps)) * lats_ref[pl.ds(r, 1), :]) / jnp.tile(l_tot, (1, reps))
        om_full = _dot(o_lat.astype(BF16), wuv_ref[...])
        row8 = lax.broadcasted_iota(jnp.int32, (HEADS, MLA_V), 0)
        om = jnp.zeros((HEADS, MLA_V), F32)
        for h in range(HEADS):
            om = jnp.where(row8 == h, om_full[:, MLA_V * h:MLA_V * (h + 1)], om)
        om_ref[...] = om

        lane = lax.broadcasted_iota(jnp.int32, (DIFF_V, LANES), 1)
        cols = jnp.zeros((DIFF_V, LANES), F32)
        for idx in range(n_maps):
            cols = jnp.where(lane == idx, jnp.sum(accd[idx], axis=1, keepdims=True), cols)
        o_rows = jnp.concatenate([cols, jnp.zeros((LANES - DIFF_V, LANES), F32)], axis=0).T
        w_d = jnp.exp2(ss[HEADS:HEADS + n_maps] - md[...])
        l_d = ld[...] + w_d
        vd_self = vds_ref[...]
        o1 = (o_rows[0:HEADS, 0:DIFF_V] + w_d[0:HEADS, 0:DIFF_V] * vd_self) / l_d[0:HEADS, 0:DIFF_V]
        o2 = (o_rows[HEADS:n_maps, 0:DIFF_V] + w_d[HEADS:, 0:DIFF_V] * vd_self) / l_d[HEADS:, 0:DIFF_V]
        od_ref[...] = o1 - _lambda(dl_ref[...]) * o2

    @pl.when(step == n_steps - 1)
    def _():
        for k in range(ahead):
            for g in range(g_n):
                for c in page_copies(k, (n_steps + k) % DEC_BUFS, g):
                    c.wait()


def _back_kernel(raw_diff, x_ref, om_ref, od_ref, g_ref, lng_ref, lnb_ref, subln_ref, wbm_ref, wbd_ref,
                 wout_ref, ln1g_ref, ln1b_ref, wup_ref, wdn_ref, ln2g_ref, ln2b_ref, y_ref):
    h = _layer_norm(x_ref[...], lng_ref[...], lnb_ref[...])
    od = od_ref[...]
    if raw_diff:
        g128 = subln_ref[...]
        od = jnp.concatenate(
            [_sub_ln_slab(od[:, LANES * s:LANES * (s + 1)], g128) for s in range(od.shape[1] // LANES)], axis=1)
    a = _dot(om_ref[...].astype(BF16), wbm_ref[...])
    b = _dot(od.astype(BF16), wbd_ref[...])
    g = g_ref[...]
    mix = _dot((g[:, :D_MODEL] * a + g[:, D_MODEL:] * b).astype(BF16), wout_ref[...])
    x1 = _layer_norm(DEEPNORM_ALPHA * h + mix, ln1g_ref[...], ln1b_ref[...])
    up = jnp.maximum(_dot(x1.astype(BF16), wup_ref[...]), 0.0)
    f = _dot((up * up).astype(BF16), wdn_ref[...])
    y_ref[...] = _layer_norm(DEEPNORM_ALPHA * x1 + f, ln2g_ref[...], ln2b_ref[...])


def _const_spec(shape):
    nd = len(shape)
    return pl.BlockSpec(shape, lambda *_: (0,) * nd, pipeline_mode=pl.Buffered(1))


def _prompt_attention(q_mla, k_mla, v_mla, km_meta, vm_meta, q_diff, k_diff, v_diff, kd_meta, vd_meta, dl, subln):
    n_b, seq, _ = q_mla.shape
    grid = (n_b, HEADS // 2, seq // TQ)
    sem = ("arbitrary", "arbitrary", "arbitrary")
    q_spec = lambda wdt: pl.BlockSpec((None, TQ, wdt), lambda b, p, i: (b, i, p))
    kv_spec = lambda wdt: pl.BlockSpec((None, seq, wdt), lambda b, p, i: (b, 0, p))
    meta_spec = lambda wdt: pl.BlockSpec((LANES, wdt), lambda b, p, i: (0, p))
    o_spec = pl.BlockSpec((None, TQ, LANES), lambda b, p, i: (b, i, p))
    o_shape = jax.ShapeDtypeStruct((n_b, seq, HEADS * MLA_V), BF16)
    state = lambda rows: [pltpu.VMEM((rows, LANES), F32) for _ in range(3)]

    o_mla = pl.pallas_call(
        _attn_mla_kernel,
        grid=grid,
        in_specs=[q_spec(2 * SLAB), kv_spec(2 * SLAB), kv_spec(LANES), meta_spec(2 * SLAB), meta_spec(LANES)],
        out_specs=o_spec,
        out_shape=o_shape,
        scratch_shapes=state(2 * TQ),
        compiler_params=_cparams(sem, 40),
        name="attn_mla",
    )(q_mla, k_mla, v_mla, km_meta, vm_meta)

    o_diff = pl.pallas_call(
        _attn_diff_kernel,
        grid=grid,
        in_specs=[q_spec(LANES), kv_spec(LANES), kv_spec(LANES), meta_spec(LANES), meta_spec(LANES),
                  pl.BlockSpec(dl.shape, lambda b, p, i: (0, 0)), pl.BlockSpec(subln.shape, lambda b, p, i: (0, 0))],
        out_specs=o_spec,
        out_shape=o_shape,
        scratch_shapes=[pltpu.VMEM((4 * TQ, LANES), BF16)] + state(4 * TQ),
        compiler_params=_cparams(sem, 40),
        name="attn_diff",
    )(q_diff, k_diff, v_diff, kd_meta, vd_meta, dl, subln)
    return o_mla, o_diff


def _rope_tables(pos):
    inv = ROPE_THETA ** (-jnp.arange(HALF, dtype=F32) / HALF)
    ang = pos.astype(F32)[:, None] * inv[None, :]
    reps = LANES // HALF
    return jnp.tile(jnp.cos(ang), (1, reps)), jnp.tile(jnp.sin(ang), (1, reps))


def kernel(x_prompt, x_sample, cache_mla_latent, cache_mla_krope, cache_diff_k, cache_diff_v, page_table,
           meta_tokens, ln_in_g, ln_in_b, w_in, b_gate, mla_q_norm, w_uq, mla_kv_norm, w_uk, w_uv,
           diff_lambda, diff_subln, w_br_mla, w_br_diff, w_out, ln1_g, ln1_b, w_up, w_down, ln2_g, ln2_b):
    n_b, seq, d = x_prompt.shape
    n_dec = x_sample.shape[0]
    n_pages = page_table.shape[1]
    l_all = seq + N_META_TOK
    nq = seq // TM
    layer = 0

    w = w_in[layer]
    win = jnp.concatenate(
        [w[:, 0:512], w[:, 544:4128], w[:, 512:544], jnp.zeros((d, C_END - C_KR - ROPE), F32)],
        axis=1).astype(BF16)
    wuq = jnp.pad(w_uq[layer].reshape(Q_RANK, HEADS, NOPE + ROPE),
                  ((0, 0), (0, 0), (0, SLAB - NOPE - ROPE))).reshape(Q_RANK, HEADS * SLAB).astype(BF16)
    wuk = jnp.pad(w_uk[layer], ((0, 0), (0, 0), (0, SLAB - NOPE))).reshape(KV_RANK, HEADS * SLAB)
    wuv = w_uv[layer].reshape(KV_RANK, HEADS * MLA_V)
    wkv = jnp.concatenate([wuk, wuv], axis=1).astype(BF16)
    wuv_b = wuv.astype(BF16)
    row = lambda v: v.reshape(1, -1)
    lng, lnb = row(ln_in_g), row(ln_in_b)
    bg, qn, kvn = row(b_gate[layer]), row(mla_q_norm[layer]), row(mla_kv_norm[layer])
    subln = row(jnp.tile(diff_subln[layer], LANES // DIFF_V))
    dl = diff_lambda[layer].astype(F32)
    front_params = (lng, lnb, win, bg, qn, wuq, kvn, wkv)
    front_param_specs = [_const_spec(p.shape) for p in front_params]

    cos_p, sin_p = _rope_tables(jnp.arange(N_META_TOK, l_all))
    cos_m, sin_m = _rope_tables(jnp.arange(LANES))
    cos_s, sin_s = _rope_tables(jnp.full((n_dec,), n_pages * PAGE))

    tok = lambda wdt, dt: jax.ShapeDtypeStruct((n_b, seq, wdt), dt)
    cache_shape = lambda wdt: jax.ShapeDtypeStruct((DEPTH, n_b, l_all, wdt), F32)
    tok_spec = lambda wdt: pl.BlockSpec((None, TM, wdt), lambda b, i: (b, i, 0))
    cache_spec = lambda wdt: pl.BlockSpec((pl.Element(1), pl.Element(1), pl.Element(TM), pl.Element(wdt)),
                                          lambda b, i: (0, b, pl.multiple_of(N_META_TOK + i * TM, N_META_TOK), 0))
    (lat_p, kr_p, kd_p, vd_p, q_mla, k_mla, v_mla, q_diff, k_diff, v_diff, gates_p) = pl.pallas_call(
        _front_prompt_kernel,
        grid=(n_b, nq),
        in_specs=[tok_spec(d), pl.BlockSpec((TM, LANES), lambda b, i: (i, 0)),
                  pl.BlockSpec((TM, LANES), lambda b, i: (i, 0))] + front_param_specs,
        out_specs=[cache_spec(KV_RANK), cache_spec(ROPE), cache_spec(512), cache_spec(512),
                   tok_spec(HEADS * SLAB), tok_spec(HEADS * SLAB), tok_spec(512),
                   tok_spec(512), tok_spec(512), tok_spec(512), tok_spec(2 * d)],
        out_shape=[cache_shape(KV_RANK), cache_shape(ROPE), cache_shape(512), cache_shape(512),
                   tok(HEADS * SLAB, BF16), tok(HEADS * SLAB, BF16), tok(512, BF16),
                   tok(512, BF16), tok(512, BF16), tok(512, BF16), tok(2 * d, F32)],
        compiler_params=_cparams(("arbitrary", "arbitrary"), 48),
        name="front_prompt",
    )(x_prompt, cos_p, sin_p, *front_params)

    meta_pad = jnp.pad(meta_tokens.astype(F32), ((0, LANES - N_META_TOK), (0, 0)))
    any_spec = pl.BlockSpec(memory_space=pl.ANY)
    meta_cache_spec = lambda wdt: pl.BlockSpec((None, None, N_META_TOK, wdt), lambda b: (0, b, 0, 0))
    mrow = lambda wdt: jax.ShapeDtypeStruct((LANES, wdt), BF16)
    mrow_spec = lambda wdt: pl.BlockSpec((LANES, wdt), lambda b: (0, 0))
    n_fp = 3 + len(front_params)
    (lat_p, kr_p, kd_p, vd_p, km_meta, vm_meta, kd_meta, vd_meta) = pl.pallas_call(
        _front_meta_kernel,
        grid=(n_b,),
        in_specs=[_const_spec((LANES, d)), _const_spec((LANES, LANES)), _const_spec((LANES, LANES))]
        + front_param_specs + [any_spec] * 4,
        out_specs=[meta_cache_spec(KV_RANK), meta_cache_spec(ROPE), meta_cache_spec(512), meta_cache_spec(512),
                   mrow_spec(HEADS * SLAB), mrow_spec(512), mrow_spec(512), mrow_spec(512)],
        out_shape=[cache_shape(KV_RANK), cache_shape(ROPE), cache_shape(512), cache_shape(512),
                   mrow(HEADS * SLAB), mrow(512), mrow(512), mrow(512)],
        input_output_aliases={n_fp: 0, n_fp + 1: 1, n_fp + 2: 2, n_fp + 3: 3},
        compiler_params=_cparams(("arbitrary",), 40),
        name="front_meta",
    )(meta_pad, cos_m, sin_m, *front_params, lat_p, kr_p, kd_p, vd_p)

    xs = x_sample.reshape(n_dec, d)
    sds = lambda shape, dt: jax.ShapeDtypeStruct(shape, dt)
    full = lambda shape: pl.BlockSpec(shape, lambda i: (0,) * len(shape))
    s_shapes = [(n_dec, KV_RANK), (n_dec, ROPE), (n_dec, 512), (n_dec, 512), (n_dec, 2 * d),
                (HEADS, n_dec, KV_RANK), (HEADS, n_dec, ROPE), (n_dec, 512), (DEC_ROWS, n_dec, LANES)]
    (lat_s, kr_s, kd_s, vd_s, gates_s, qlat_s, qrope_s, qd_s, ss_s) = pl.pallas_call(
        _front_sample_kernel,
        grid=(1,),
        in_specs=[_const_spec((n_dec, d)), _const_spec((n_dec, LANES)), _const_spec((n_dec, LANES))]
        + front_param_specs,
        out_specs=[full(s) for s in s_shapes],
        out_shape=[sds(s, F32) for s in s_shapes],
        compiler_params=_cparams(("arbitrary",), 40),
        name="front_sample",
    )(xs, cos_s, sin_s, *front_params)

    o_mla, o_diff = _prompt_attention(q_mla, k_mla, v_mla, km_meta, vm_meta,
                                      q_diff, k_diff, v_diff, kd_meta, vd_meta, dl, subln)

    n_t = n_pages // DEC_G
    pt_flat = page_table.reshape(-1).astype(jnp.int32)
    kr_t = jnp.transpose(cache_mla_krope, (0, 1, 3, 2))
    kd_t = jnp.transpose(cache_diff_k, (0, 1, 3, 4, 2))
    vd_t = jnp.transpose(cache_diff_v, (0, 1, 3, 4, 2))
    any_spec = pl.BlockSpec(memory_space=pl.ANY)
    qlat_b = jnp.transpose(qlat_s, (1, 0, 2))
    qrope_b = jnp.transpose(qrope_s, (1, 0, 2))
    ss_b = jnp.transpose(ss_s, (1, 0, 2))
    per_seq = lambda rows, wdt: pl.BlockSpec((None, rows, wdt), lambda b, t, pt: (b, 0, 0))
    group8 = lambda wdt: pl.BlockSpec((SUBLANES, wdt), lambda b, t, pt: (b // SUBLANES, 0))
    dec_in_specs = (
        [per_seq(HEADS, KV_RANK), per_seq(HEADS, ROPE), group8(512), per_seq(DEC_ROWS, LANES),
         group8(KV_RANK), per_seq(HEADS, DIFF_V),
         pl.BlockSpec(dl.shape, lambda b, t, pt: (0, 0)), pl.BlockSpec(wuv_b.shape, lambda b, t, pt: (0, 0))]
        + [any_spec] * 4)
    dec_out_spec = pl.BlockSpec((None, HEADS, MLA_V), lambda b, t, pt: (b, 0, 0))
    om_s, od_s = pl.pallas_call(
        functools.partial(_decode_kernel, n_dec * n_t, n_t),
        grid_spec=pltpu.PrefetchScalarGridSpec(
            num_scalar_prefetch=1,
            grid=(n_dec, n_t),
            in_specs=dec_in_specs,
            out_specs=[dec_out_spec, dec_out_spec],
            scratch_shapes=[pltpu.VMEM((DEC_BUFS, DEC_G, PAGE, KV_RANK), F32),
                            pltpu.VMEM((DEC_BUFS, DEC_G, ROPE, PAGE), F32),
                            pltpu.VMEM((DEC_BUFS, DEC_G, HEADS, 2 * DIFF_DIM, PAGE), F32),
                            pltpu.VMEM((DEC_BUFS, DEC_G, HEADS, DIFF_V, PAGE), F32),
                            pltpu.SemaphoreType.DMA((DEC_BUFS, 4)),
                            pltpu.VMEM((HEADS, 2 * DIFF_DIM, LANES), F32),
                            pltpu.VMEM((HEADS, LANES), F32), pltpu.VMEM((HEADS, LANES), F32),
                            pltpu.VMEM((HEADS, KV_RANK), F32),
                            pltpu.VMEM((2 * HEADS, LANES), F32), pltpu.VMEM((2 * HEADS, LANES), F32),
                            pltpu.VMEM((2 * HEADS, DIFF_V, LANES), F32)]),
        out_shape=[jax.ShapeDtypeStruct((n_dec, HEADS, MLA_V), F32)] * 2,
        compiler_params=_cparams(("arbitrary", "arbitrary"), 48),
        name="decode_attn",
    )(pt_flat, qlat_b, qrope_b, qd_s, ss_b, lat_s, vd_s.reshape(n_dec, HEADS, DIFF_V), dl, wuv_b,
      cache_mla_latent, kr_t, kd_t, vd_t)

    back_params = (lng, lnb, subln, w_br_mla[layer].astype(BF16), w_br_diff[layer].astype(BF16),
                   w_out[layer].astype(BF16), row(ln1_g[layer]), row(ln1_b[layer]), w_up[layer].astype(BF16),
                   w_down[layer].astype(BF16), row(ln2_g[layer]), row(ln2_b[layer]))
    back_param_specs = [_const_spec(p.shape) for p in back_params]
    y_prompt = pl.pallas_call(
        functools.partial(_back_kernel, False),
        grid=(n_b, nq),
        in_specs=[tok_spec(d), tok_spec(512), tok_spec(512), tok_spec(2 * d)] + back_param_specs,
        out_specs=tok_spec(d),
        out_shape=jax.ShapeDtypeStruct((n_b, seq, d), F32),
        compiler_params=_cparams(("arbitrary", "arbitrary"), 56),
        name="back_prompt",
    )(x_prompt, o_mla, o_diff, gates_p, *back_params)

    y_sample = pl.pallas_call(
        functools.partial(_back_kernel, True),
        grid=(1,),
        in_specs=[full((n_dec, d)), full((n_dec, 512)), full((n_dec, 512)), full((n_dec, 2 * d))]
        + back_param_specs,
        out_specs=full((n_dec, d)),
        out_shape=jax.ShapeDtypeStruct((n_dec, d), F32),
        compiler_params=_cparams(("arbitrary",), 56),
        name="back_sample",
    )(xs, om_s.reshape(n_dec, 512), od_s.reshape(n_dec, 512), gates_s, *back_params)

    head_shape = lambda a, lead, v: a.reshape(lead + (HEADS, v))
    return (y_prompt, y_sample.reshape(n_dec, 1, d),
            lat_p, kr_p,
            head_shape(kd_p, (DEPTH, n_b, l_all), 2 * DIFF_DIM), head_shape(vd_p, (DEPTH, n_b, l_all), DIFF_V),
            lat_s.reshape(DEPTH, n_dec, 1, KV_RANK), kr_s.reshape(DEPTH, n_dec, 1, ROPE),
            head_shape(kd_s, (DEPTH, n_dec, 1), 2 * DIFF_DIM), head_shape(vd_s, (DEPTH, n_dec, 1), DIFF_V))
```

```python
import functools
import math

import jax
import jax.numpy as jnp
from jax import lax
from jax.experimental import pallas as pl
from jax.experimental.pallas import tpu as pltpu

F32 = jnp.float32
BF16 = jnp.bfloat16

D_MODEL = 1024
N_META_TOK = 16
PAGE = 128
HEADS = 8
Q_RANK = 256
KV_RANK = 256
NOPE = 64
ROPE = 32
MLA_V = 64
DIFF_DIM = 32
DIFF_V = 64
D_FF = 4 * D_MODEL
DEPTH = 1
ROPE_THETA = 10000.0
LN_EPS = 1e-5
RMS_EPS = 1e-6
LOG2E = math.log2(math.e)
MLA_SCALE = (NOPE + ROPE) ** -0.5 * LOG2E
DIFF_SCALE = DIFF_DIM ** -0.5 * LOG2E
DEEPNORM_ALPHA = (2 * DEPTH) ** 0.25
LAM_INIT = 0.8 - 0.6 * math.exp(-0.3 * 0)

LANES = 128
SUBLANES = 8
SLAB = 128
HALF = DIFF_DIM // 2

C_CQ, C_CKV, C_DQ, C_DK, C_DV, C_G, C_KR, C_END = 0, 256, 512, 1024, 1536, 2048, 4096, 4224

TM = 256
TQ_MLA, TK_MLA = 1024, 1024
TQ_DIFF, TK_DIFF = 512, 1024
DEC_G = 8
DEC_BUFS = 4
DEC_ROWS = 32
NEG_INF = float("-inf")


def _cparams(sem, vmem_mb):
    return pltpu.CompilerParams(dimension_semantics=sem, vmem_limit_bytes=vmem_mb * 1024 * 1024)


def _dot(a, b):
    return jnp.dot(a, b, preferred_element_type=F32)


def _dot_nt(a, b):
    return lax.dot_general(a, b, (((1,), (1,)), ((), ())), preferred_element_type=F32)


def _layer_norm(x, g, b):
    mu = jnp.mean(x, axis=-1, keepdims=True)
    xc = x - mu
    var = jnp.mean(xc * xc, axis=-1, keepdims=True)
    return xc * lax.rsqrt(var + LN_EPS) * g + b


def _rms_norm(x, g):
    return x * lax.rsqrt(jnp.mean(x * x, axis=-1, keepdims=True) + RMS_EPS) * g


def _rope_slabs(x, c, s_up, s_dn):
    outs = []
    for s in range(x.shape[1] // LANES):
        xs = x[:, LANES * s:LANES * (s + 1)]
        outs.append(xs * c + pltpu.roll(xs, HALF, 1) * s_up + pltpu.roll(xs, LANES - HALF, 1) * s_dn)
    return outs[0] if len(outs) == 1 else jnp.concatenate(outs, axis=1)


def _sub_ln_slab(x, g128):
    lane = lax.broadcasted_iota(jnp.int32, x.shape, 1)
    lo = lane < DIFF_V
    x2 = x * x
    s0 = jnp.sum(jnp.where(lo, x2, 0.0), axis=1, keepdims=True)
    s1 = jnp.sum(jnp.where(lo, 0.0, x2), axis=1, keepdims=True)
    ms = jnp.where(lo, s0, s1) * (1.0 / DIFF_V)
    return x * lax.rsqrt(ms + RMS_EPS) * g128 * (1.0 - LAM_INIT)


def _lambda(dl):
    a = jnp.sum(dl[0:1, :] * dl[1:2, :], axis=1, keepdims=True)
    b = jnp.sum(dl[2:3, :] * dl[3:4, :], axis=1, keepdims=True)
    return jnp.exp(a) - jnp.exp(b) + LAM_INIT


def _front_core(x, cosb, sinb, lng, lnb, win_ref, bg, qn, wuq_ref, kvn, wkv_ref):
    tm = x.shape[0]
    hb = _layer_norm(x, lng, lnb).astype(BF16)

    lane = lax.broadcasted_iota(jnp.int32, (tm, LANES), 1)
    first = (lane & (DIFF_DIM - 1)) < HALF
    zero = jnp.zeros_like(sinb)
    s_up_d = jnp.where(first, zero, sinb)
    s_dn_d = jnp.where(first, -sinb, zero)
    c_q = jnp.where(lane < NOPE, 1.0, cosb)
    s_up_q = jnp.where((lane >= NOPE + HALF) & (lane < NOPE + ROPE), sinb, zero)
    s_dn_q = jnp.where((lane >= NOPE) & (lane < NOPE + HALF), -sinb, zero)
    c_k = jnp.where(lane < ROPE, cosb, zero)
    s_up_k = jnp.where((lane >= HALF) & (lane < ROPE), sinb, zero)
    s_dn_k = jnp.where(lane < HALF, -sinb, zero)

    ya = _dot(hb, win_ref[:, C_CQ:C_DQ])
    cqn = _rms_norm(ya[:, :Q_RANK], qn).astype(BF16)
    q = _rope_slabs(_dot(cqn, wuq_ref[...]), c_q, s_up_q, s_dn_q) * MLA_SCALE
    lat = _rms_norm(ya[:, Q_RANK:], kvn)
    kv = _dot(lat.astype(BF16), wkv_ref[...])
    ykr = _dot(hb, win_ref[:, C_KR:C_END])
    kr = ykr * c_k + pltpu.roll(ykr, HALF, 1) * s_up_k + pltpu.roll(ykr, LANES - HALF, 1) * s_dn_k
    kr_at_rope = pltpu.roll(kr, NOPE, 1)
    k_mla = jnp.concatenate(
        [kv[:, SLAB * s:SLAB * (s + 1)] + kr_at_rope for s in range(HEADS)], axis=1)
    v_mla = kv[:, HEADS * SLAB:]

    yd = _dot(hb, win_ref[:, C_DQ:C_G])
    qd = _rope_slabs(yd[:, 0:512], cosb, s_up_d, s_dn_d) * DIFF_SCALE
    kd = _rope_slabs(yd[:, 512:1024], cosb, s_up_d, s_dn_d)
    vd = yd[:, 1024:1536]
    gates = jax.nn.sigmoid(_dot(hb, win_ref[:, C_G:C_KR]) + bg)
    return dict(q=q, lat=lat, kr=kr, k_mla=k_mla, v_mla=v_mla, qd=qd, kd=kd, vd=vd, gates=gates)


def _front_prompt_kernel(x_ref, cos_ref, sin_ref, lng_ref, lnb_ref, win_ref, bg_ref, qn_ref, wuq_ref,
                         kvn_ref, wkv_ref,
                         lat_o, kr_o, kd_o, vd_o, qm_o, km_o, vm_o, qd_o, kdb_o, vdb_o, g_o):
    r = _front_core(x_ref[...], cos_ref[...], sin_ref[...], lng_ref[...], lnb_ref[...], win_ref,
                    bg_ref[...], qn_ref[...], wuq_ref, kvn_ref[...], wkv_ref)
    lat_o[0, 0] = r["lat"]
    kr_o[0, 0] = r["kr"][:, :ROPE]
    kd_o[0, 0] = r["kd"]
    vd_o[0, 0] = r["vd"]
    qm_o[...] = r["q"].astype(BF16)
    km_o[...] = r["k_mla"].astype(BF16)
    vm_o[...] = r["v_mla"].astype(BF16)
    qd_o[...] = r["qd"].astype(BF16)
    kdb_o[...] = r["kd"].astype(BF16)
    vdb_o[...] = r["vd"].astype(BF16)
    g_o[...] = r["gates"]


def _front_meta_kernel(x_ref, cos_ref, sin_ref, lng_ref, lnb_ref, win_ref, bg_ref, qn_ref, wuq_ref,
                       kvn_ref, wkv_ref, lat_in, kr_in, kd_in, vd_in,
                       lat_o, kr_o, kd_o, vd_o, km_o, vm_o, kdb_o, vdb_o):
    del lat_in, kr_in, kd_in, vd_in
    r = _front_core(x_ref[...], cos_ref[...], sin_ref[...], lng_ref[...], lnb_ref[...], win_ref,
                    bg_ref[...], qn_ref[...], wuq_ref, kvn_ref[...], wkv_ref)
    n = N_META_TOK
    lat_o[...] = r["lat"][:n]
    kr_o[...] = r["kr"][:n, :ROPE]
    kd_o[...] = r["kd"][:n]
    vd_o[...] = r["vd"][:n]
    km_o[...] = r["k_mla"].astype(BF16)
    vm_o[...] = r["v_mla"].astype(BF16)
    kdb_o[...] = r["kd"].astype(BF16)
    vdb_o[...] = r["vd"].astype(BF16)


def _front_sample_kernel(x_ref, cos_ref, sin_ref, lng_ref, lnb_ref, win_ref, bg_ref, qn_ref, wuq_ref,
                         kvn_ref, wkv_ref,
                         lat_o, kr_o, kd_o, vd_o, g_o, qlat_o, qrope_o, qd_o, ss_o):
    r = _front_core(x_ref[...], cos_ref[...], sin_ref[...], lng_ref[...], lnb_ref[...], win_ref,
                    bg_ref[...], qn_ref[...], wuq_ref, kvn_ref[...], wkv_ref)
    nb = x_ref.shape[0]
    lat, kr, kd, q, qd = r["lat"], r["kr"], r["kd"], r["q"], r["qd"]
    lat_o[...] = lat
    kr_o[...] = kr[:, :ROPE]
    kd_o[...] = kd
    vd_o[...] = r["vd"]
    g_o[...] = r["gates"]
    qd_o[...] = qd

    lane = lax.broadcasted_iota(jnp.int32, (nb, LANES), 1)
    kr_at_rope = pltpu.roll(kr, NOPE, 1)
    rope_lanes = (lane >= NOPE) & (lane < NOPE + ROPE)
    rep = lambda col: jnp.broadcast_to(col, (nb, LANES))
    for h in range(HEADS):
        q_slab = q[:, SLAB * h:SLAB * (h + 1)]
        q_lat = _dot_nt(q_slab.astype(BF16), wkv_ref[:, SLAB * h:SLAB * (h + 1)])
        qlat_o[h] = q_lat
        qrope_o[h] = pltpu.roll(q_slab, LANES - NOPE, 1)[:, :ROPE]
        s_h = (jnp.sum(q_lat * lat, axis=1, keepdims=True)
               + jnp.sum(jnp.where(rope_lanes, q_slab * kr_at_rope, 0.0), axis=1, keepdims=True))
        ss_o[h] = rep(s_h)
    lane4 = lax.broadcasted_iota(jnp.int32, (nb, 4 * LANES), 1)
    prod = qd * kd
    for h in range(HEADS):
        for m in range(2):
            lo = h * 2 * DIFF_DIM + m * DIFF_DIM
            seg = (lane4 >= lo) & (lane4 < lo + DIFF_DIM)
            ss_o[HEADS + HEADS * m + h] = rep(jnp.sum(jnp.where(seg, prod, 0.0), axis=1, keepdims=True))
    for c in range(3 * HEADS, DEC_ROWS):
        ss_o[c] = jnp.zeros((nb, LANES), F32)


def _softmax_step(blocks, first, m_scr, l_scr, acc_scr):
    m_cur = None
    for s, _ in blocks:
        m_b = jnp.max(s, axis=1, keepdims=True)
        m_cur = m_b if m_cur is None else jnp.maximum(m_cur, m_b)
    if first:
        m_new = jnp.broadcast_to(m_cur, m_scr.shape)
    else:
        m_prev = m_scr[...]
        m_new = jnp.maximum(m_prev, m_cur)
        corr = jnp.exp2(m_prev - m_new)
    l_cur = None
    pv = None
    for s, v in blocks:
        reps = s.shape[1] // LANES
        p = jnp.exp2(s - (jnp.tile(m_new, (1, reps)) if reps > 1 else m_new))
        l_b = jnp.sum(p, axis=1, keepdims=True)
        pv_b = _dot(p.astype(BF16), v)
        l_cur = l_b if l_cur is None else l_cur + l_b
        pv = pv_b if pv is None else pv + pv_b
    if first:
        l_scr[...] = jnp.broadcast_to(l_cur, l_scr.shape)
        acc_scr[...] = pv
    else:
        l_scr[...] = corr * l_scr[...] + l_cur
        acc_scr[...] = corr * acc_scr[...] + pv
    m_scr[...] = m_new


def _scores(groups, k_at):
    parts = [_dot_nt(q, k_at(ko)) for q, ko in groups]
    return parts[0] if len(parts) == 1 else jnp.concatenate(parts, axis=0)


def _edge_step(qi, tq, tk, groups, k_ref, v_ref, km_ref, vm_ref, m_scr, l_scr, acc_scr):
    rows = sum(q.shape[0] for q, _ in groups)
    ratio = tk // tq

    def edge_blocks(n_rem):
        col = lax.broadcasted_iota(jnp.int32, (rows, LANES), 1)
        s = _scores(groups, lambda ko: km_ref[:, ko:ko + LANES])
        blocks = [(jnp.where(col < N_META_TOK, s, NEG_INF), vm_ref[...])]
        for j in range(n_rem):
            start = pl.multiple_of((qi - n_rem + j) * tq, tq)
            blocks.append((_scores(groups, lambda ko: k_ref[pl.ds(start, tq), ko:ko + LANES]),
                           v_ref[pl.ds(start, tq), :]))
        start = pl.multiple_of(qi * tq, tq)
        row = lax.broadcasted_iota(jnp.int32, (rows, tq), 0)
        colq = lax.broadcasted_iota(jnp.int32, (rows, tq), 1)
        s = _scores(groups, lambda ko: k_ref[pl.ds(start, tq), ko:ko + LANES])
        blocks.append((jnp.where(colq <= (row & (tq - 1)), s, NEG_INF), v_ref[pl.ds(start, tq), :]))
        return blocks

    if ratio == 1:
        _softmax_step(edge_blocks(0), True, m_scr, l_scr, acc_scr)
    else:
        for n_rem in range(ratio):
            @pl.when(qi % ratio == n_rem)
            def _():
                _softmax_step(edge_blocks(n_rem), True, m_scr, l_scr, acc_scr)


def _full_step(c, tk, groups, k_ref, v_ref, m_scr, l_scr, acc_scr):
    start = pl.multiple_of(c * tk, tk)
    s = _scores(groups, lambda ko: k_ref[pl.ds(start, tk), ko:ko + LANES])
    _softmax_step([(s, v_ref[pl.ds(start, tk), :])], False, m_scr, l_scr, acc_scr)


def _causal_sweep(qi, tq, tk, groups, k_ref, v_ref, km_ref, vm_ref, m_scr, l_scr, acc_scr):
    _edge_step(qi, tq, tk, groups, k_ref, v_ref, km_ref, vm_ref, m_scr, l_scr, acc_scr)

    def full_chunk(c, carry):
        _full_step(c, tk, groups, k_ref, v_ref, m_scr, l_scr, acc_scr)
        return carry

    lax.fori_loop(0, qi // (tk // tq), full_chunk, 0)


def _attn_mla_kernel(q_ref, k_ref, v_ref, km_ref, vm_ref, o_ref, m_scr, l_scr, acc_scr):
    qi = pl.program_id(2)
    tq = q_ref.shape[0]
    groups = [(q_ref[:, SLAB * hh:SLAB * (hh + 1)], SLAB * hh) for hh in range(2)]
    _causal_sweep(qi, tq, TK_MLA, groups, k_ref, v_ref, km_ref, vm_ref, m_scr, l_scr, acc_scr)
    lane = lax.broadcasted_iota(jnp.int32, (tq, LANES), 1)
    o = acc_scr[...] / l_scr[...]
    o_ref[...] = jnp.where(lane < MLA_V, o[0:tq], o[tq:2 * tq]).astype(o_ref.dtype)


def _attn_diff_kernel(q_ref, k_ref, v_ref, km_ref, vm_ref, dl_ref, g_ref, o_ref, qs_scr, m_scr, l_scr, acc_scr):
    qi = pl.program_id(2)
    tq = q_ref.shape[0]
    q = q_ref[...]
    lane = lax.broadcasted_iota(jnp.int32, (tq, LANES), 1)
    zero = jnp.zeros_like(q)
    for j in range(4):
        qs_scr[j * tq:(j + 1) * tq, :] = jnp.where((lane >= DIFF_DIM * j) & (lane < DIFF_DIM * (j + 1)), q, zero)
    _causal_sweep(qi, tq, TK_DIFF, [(qs_scr[...], 0)], k_ref, v_ref, km_ref, vm_ref, m_scr, l_scr, acc_scr)
    lam = _lambda(dl_ref[...])
    o = acc_scr[...] / l_scr[...]
    o0 = o[0:tq] - lam * o[tq:2 * tq]
    o1 = o[2 * tq:3 * tq] - lam * o[3 * tq:4 * tq]
    o_ref[...] = _sub_ln_slab(jnp.where(lane < DIFF_V, o0, o1), g_ref[...]).astype(o_ref.dtype)


def _decode_step(step, n_steps, n_t, pt_ref, qlat_ref, qrope_ref, qd_ref, ss_ref, lats_ref, vds_ref, dl_ref,
                 wuv_ref, lat_hbm, kr_hbm, kt_hbm, vt_hbm, om_ref, od_ref,
                 lat_buf, kr_buf, kt_buf, vt_buf, sem, qcol, ml, ll, accl, md, ld, accd):
    g_n = DEC_G
    b = step // n_t
    t = lax.rem(step, n_t)
    r = lax.rem(b, SUBLANES)
    n_maps = 2 * HEADS

    def page_copies(src_step, slot_, g):
        page = pt_ref[src_step * g_n + g]
        return (pltpu.make_async_copy(lat_hbm.at[0, page], lat_buf.at[slot_, g], sem.at[slot_, 0]),
                pltpu.make_async_copy(kr_hbm.at[0, page], kr_buf.at[slot_, g], sem.at[slot_, 1]),
                pltpu.make_async_copy(kt_hbm.at[0, page], kt_buf.at[slot_, g], sem.at[slot_, 2]),
                pltpu.make_async_copy(vt_hbm.at[0, page], vt_buf.at[slot_, g], sem.at[slot_, 3]))

    ahead = DEC_BUFS - 1
    slot = lax.rem(step, DEC_BUFS)
    nxt_step = lax.rem(step + ahead, n_steps)
    nxt_slot = lax.rem(step + ahead, DEC_BUFS)

    @pl.when(step == 0)
    def _():
        for s0 in range(ahead):
            for g in range(g_n):
                for c in page_copies(s0, s0, g):
                    c.start()

    for g in range(g_n):
        for c in page_copies(step, slot, g):
            c.wait()
    for g in range(g_n):
        for c in page_copies(nxt_step, nxt_slot, g):
            c.start()

    lat_refs = [lat_buf.at[slot, g] for g in range(g_n)]
    kr_refs = [kr_buf.at[slot, g] for g in range(g_n)]
    kt_refs = [kt_buf.at[slot, g] for g in range(g_n)]
    vt_refs = [vt_buf.at[slot, g] for g in range(g_n)]

    @pl.when(t == 0)
    def _():
        qrow = qd_ref[pl.ds(r, 1), :]
        for s in range(HEADS // 2):
            cols = jnp.broadcast_to(qrow[:, LANES * s:LANES * (s + 1)], (LANES, LANES)).T
            qcol[2 * s] = cols[0:2 * DIFF_DIM]
            qcol[2 * s + 1] = cols[2 * DIFF_DIM:]
        ss = ss_ref[...]
        ml[...] = ss[0:HEADS]
        md[...] = ss[HEADS:HEADS + n_maps]
        ll[...] = jnp.zeros(ll.shape, F32)
        ld[...] = jnp.zeros(ld.shape, F32)
        accl[...] = jnp.zeros(accl.shape, F32)
        accd[...] = jnp.zeros(accd.shape, F32)

    qlat = qlat_ref[...].astype(BF16)
    qrope = qrope_ref[...].astype(BF16)
    lat_b = [ref[...].astype(BF16) for ref in lat_refs]
    s_l = [_dot_nt(qlat, lat_b[g]) + _dot(qrope, kr_refs[g][...].astype(BF16)) for g in range(g_n)]
    mx = s_l[0]
    for g in range(1, g_n):
        mx = jnp.maximum(mx, s_l[g])
    m_old = ml[...]
    m_new = jnp.maximum(m_old, jnp.max(mx, axis=1, keepdims=True))
    corr = jnp.exp2(m_old - m_new)
    psum = jnp.zeros((HEADS, LANES), F32)
    o_l = jnp.zeros((HEADS, KV_RANK), F32)
    for g in range(g_n):
        p = jnp.exp2(s_l[g] - m_new)
        psum = psum + p
        o_l = o_l + _dot(p.astype(BF16), lat_b[g])
    ll[...] = ll[...] * corr + jnp.sum(psum, axis=1, keepdims=True)
    accl[...] = accl[...] * jnp.tile(corr, (1, KV_RANK // LANES)) + o_l
    ml[...] = m_new

    for h in range(HEADS):
        qc = qcol[h]
        rows = ([], [])
        for g in range(g_n):
            prod = kt_refs[g][h] * qc
            rows[0].append(jnp.sum(prod[0:DIFF_DIM], axis=0, keepdims=True))
            rows[1].append(jnp.sum(prod[DIFF_DIM:], axis=0, keepdims=True))
        for m in range(2):
            idx = HEADS * m + h
            s_d = jnp.concatenate(rows[m], axis=0)
            m_old = md[idx:idx + 1, :]
            m_new = jnp.maximum(
                m_old, jnp.max(jnp.max(s_d, axis=1, keepdims=True), axis=0, keepdims=True))
            corr = jnp.exp2(m_old - m_new)
            p_d = jnp.exp2(s_d - m_new)
            ld[idx:idx + 1, :] = ld[idx:idx + 1, :] * corr + jnp.sum(
                jnp.sum(p_d, axis=1, keepdims=True), axis=0, keepdims=True)
            acc = accd[idx] * corr
            for g in range(g_n):
                acc = acc + vt_refs[g][h] * p_d[g:g + 1, :]
            accd[idx] = acc
            md[idx:idx + 1, :] = m_new

    @pl.when(t == n_t - 1)
    def _():
        ss = ss_ref[...]
        w_self = jnp.exp2(ss[0:HEADS] - ml[...])
        l_tot = ll[...] + w_self
        reps = KV_RANK // LANES
        o_lat = (accl[...] + jnp.tile(w_self, (1, reps)) * lats_ref[pl.ds(r, 1), :]) / jnp.tile(l_tot, (1, reps))
        om_full = _dot(o_lat.astype(BF16), wuv_ref[...])
        row8 = lax.broadcasted_iota(jnp.int32, (HEADS, MLA_V), 0)
        om = jnp.zeros((HEADS, MLA_V), F32)
        for h in range(HEADS):
            om = jnp.where(row8 == h, om_full[:, MLA_V * h:MLA_V * (h + 1)], om)
        om_ref[...] = om

        lane = lax.broadcasted_iota(jnp.int32, (DIFF_V, LANES), 1)
        cols = jnp.zeros((DIFF_V, LANES), F32)
        for idx in range(n_maps):
            cols = jnp.where(lane == idx, jnp.sum(accd[idx], axis=1, keepdims=True), cols)
        o_rows = jnp.concatenate([cols, jnp.zeros((LANES - DIFF_V, LANES), F32)], axis=0).T
        w_d = jnp.exp2(ss[HEADS:HEADS + n_maps] - md[...])
        l_d = ld[...] + w_d
        vd_self = vds_ref[...]
        o1 = (o_rows[0:HEADS, 0:DIFF_V] + w_d[0:HEADS, 0:DIFF_V] * vd_self) / l_d[0:HEADS, 0:DIFF_V]
        o2 = (o_rows[HEADS:n_maps, 0:DIFF_V] + w_d[HEADS:, 0:DIFF_V] * vd_self) / l_d[HEADS:, 0:DIFF_V]
        od_ref[...] = o1 - _lambda(dl_ref[...]) * o2

    @pl.when(step == n_steps - 1)
    def _():
        for k in range(ahead):
            for g in range(g_n):
                for c in page_copies(k, (n_steps + k) % DEC_BUFS, g):
                    c.wait()


def _decode_kernel(n_steps, n_t, *refs):
    _decode_step(pl.program_id(0) * n_t + pl.program_id(1), n_steps, n_t, *refs)


def _back_kernel(raw_diff, x_ref, om_ref, od_ref, g_ref, lng_ref, lnb_ref, subln_ref, wbm_ref, wbd_ref,
                 wout_ref, ln1g_ref, ln1b_ref, wup_ref, wdn_ref, ln2g_ref, ln2b_ref, y_ref):
    h = _layer_norm(x_ref[...], lng_ref[...], lnb_ref[...])
    od = od_ref[...]
    if raw_diff:
        g128 = subln_ref[...]
        od = jnp.concatenate(
            [_sub_ln_slab(od[:, LANES * s:LANES * (s + 1)], g128) for s in range(od.shape[1] // LANES)], axis=1)
    a = _dot(om_ref[...].astype(BF16), wbm_ref[...])
    b = _dot(od.astype(BF16), wbd_ref[...])
    g = g_ref[...]
    mix = _dot((g[:, :D_MODEL] * a + g[:, D_MODEL:] * b).astype(BF16), wout_ref[...])
    x1 = _layer_norm(DEEPNORM_ALPHA * h + mix, ln1g_ref[...], ln1b_ref[...])
    up = jnp.maximum(_dot(x1.astype(BF16), wup_ref[...]), 0.0)
    f = _dot((up * up).astype(BF16), wdn_ref[...])
    y_ref[...] = _layer_norm(DEEPNORM_ALPHA * x1 + f, ln2g_ref[...], ln2b_ref[...])


def _const_spec(shape):
    nd = len(shape)
    return pl.BlockSpec(shape, lambda *_: (0,) * nd, pipeline_mode=pl.Buffered(1))


def _prompt_attention(q_mla, k_mla, v_mla, km_meta, vm_meta, q_diff, k_diff, v_diff, kd_meta, vd_meta, dl, subln):
    n_b, seq, _ = q_mla.shape
    sem = ("arbitrary", "arbitrary", "arbitrary")
    q_spec = lambda tq, wdt: pl.BlockSpec((None, tq, wdt), lambda b, p, i: (b, i, p))
    kv_spec = lambda wdt: pl.BlockSpec((None, seq, wdt), lambda b, p, i: (b, 0, p))
    meta_spec = lambda wdt: pl.BlockSpec((LANES, wdt), lambda b, p, i: (0, p))
    o_shape = jax.ShapeDtypeStruct((n_b, seq, HEADS * MLA_V), BF16)
    state = lambda rows: [pltpu.VMEM((rows, LANES), F32) for _ in range(3)]
    tq_m, tq_d = min(TQ_MLA, seq), min(TQ_DIFF, seq)

    o_mla = pl.pallas_call(
        _attn_mla_kernel,
        grid=(n_b, HEADS // 2, seq // tq_m),
        in_specs=[q_spec(tq_m, 2 * SLAB), kv_spec(2 * SLAB), kv_spec(LANES), meta_spec(2 * SLAB), meta_spec(LANES)],
        out_specs=q_spec(tq_m, LANES),
        out_shape=o_shape,
        scratch_shapes=state(2 * tq_m),
        compiler_params=_cparams(sem, 48),
        name="attn_mla",
    )(q_mla, k_mla, v_mla, km_meta, vm_meta)

    o_diff = pl.pallas_call(
        _attn_diff_kernel,
        grid=(n_b, HEADS // 2, seq // tq_d),
        in_specs=[q_spec(tq_d, LANES), kv_spec(LANES), kv_spec(LANES), meta_spec(LANES), meta_spec(LANES),
                  pl.BlockSpec(dl.shape, lambda b, p, i: (0, 0)), pl.BlockSpec(subln.shape, lambda b, p, i: (0, 0))],
        out_specs=q_spec(tq_d, LANES),
        out_shape=o_shape,
        scratch_shapes=[pltpu.VMEM((4 * tq_d, LANES), BF16)] + state(4 * tq_d),
        compiler_params=_cparams(sem, 48),
        name="attn_diff",
    )(q_diff, k_diff, v_diff, kd_meta, vd_meta, dl, subln)
    return o_mla, o_diff


def _decode_attention(page_table, qlat_b, qrope_b, qd_s, ss_b, lat_s, vds, dl, wuv_b, cache_lat, kr_t, kd_t, vd_t):
    n_dec, n_pages = page_table.shape
    n_t = n_pages // DEC_G
    pt_flat = page_table.reshape(-1).astype(jnp.int32)
    per_seq = lambda rows, wdt: pl.BlockSpec((None, rows, wdt), lambda b, t, pt: (b, 0, 0))
    group8 = lambda wdt: pl.BlockSpec((SUBLANES, wdt), lambda b, t, pt: (b // SUBLANES, 0))
    const2 = lambda a: pl.BlockSpec(a.shape, lambda b, t, pt: (0, 0))
    any_spec = pl.BlockSpec(memory_space=pl.ANY)
    dec_out_spec = pl.BlockSpec((None, HEADS, MLA_V), lambda b, t, pt: (b, 0, 0))
    return pl.pallas_call(
        functools.partial(_decode_kernel, n_dec * n_t, n_t),
        grid_spec=pltpu.PrefetchScalarGridSpec(
            num_scalar_prefetch=1,
            grid=(n_dec, n_t),
            in_specs=[per_seq(HEADS, KV_RANK), per_seq(HEADS, ROPE), group8(512), per_seq(DEC_ROWS, LANES),
                      group8(KV_RANK), per_seq(HEADS, DIFF_V), const2(dl), const2(wuv_b)]
            + [any_spec] * 4,
            out_specs=[dec_out_spec, dec_out_spec],
            scratch_shapes=[pltpu.VMEM((DEC_BUFS, DEC_G, PAGE, KV_RANK), F32),
                            pltpu.VMEM((DEC_BUFS, DEC_G, ROPE, PAGE), F32),
                            pltpu.VMEM((DEC_BUFS, DEC_G, HEADS, 2 * DIFF_DIM, PAGE), F32),
                            pltpu.VMEM((DEC_BUFS, DEC_G, HEADS, DIFF_V, PAGE), F32),
                            pltpu.SemaphoreType.DMA((DEC_BUFS, 4)),
                            pltpu.VMEM((HEADS, 2 * DIFF_DIM, LANES), F32),
                            pltpu.VMEM((HEADS, LANES), F32), pltpu.VMEM((HEADS, LANES), F32),
                            pltpu.VMEM((HEADS, KV_RANK), F32),
                            pltpu.VMEM((2 * HEADS, LANES), F32), pltpu.VMEM((2 * HEADS, LANES), F32),
                            pltpu.VMEM((2 * HEADS, DIFF_V, LANES), F32)]),
        out_shape=[jax.ShapeDtypeStruct((n_dec, HEADS, MLA_V), F32)] * 2,
        compiler_params=_cparams(("arbitrary", "arbitrary"), 48),
        name="decode_attn",
    )(pt_flat, qlat_b, qrope_b, qd_s, ss_b, lat_s, vds, dl, wuv_b, cache_lat, kr_t, kd_t, vd_t)


def _rope_tables(pos):
    inv = ROPE_THETA ** (-jnp.arange(HALF, dtype=F32) / HALF)
    ang = pos.astype(F32)[:, None] * inv[None, :]
    reps = LANES // HALF
    return jnp.tile(jnp.cos(ang), (1, reps)), jnp.tile(jnp.sin(ang), (1, reps))


def kernel(x_prompt, x_sample, cache_mla_latent, cache_mla_krope, cache_diff_k, cache_diff_v, page_table,
           meta_tokens, ln_in_g, ln_in_b, w_in, b_gate, mla_q_norm, w_uq, mla_kv_norm, w_uk, w_uv,
           diff_lambda, diff_subln, w_br_mla, w_br_diff, w_out, ln1_g, ln1_b, w_up, w_down, ln2_g, ln2_b):
    n_b, seq, d = x_prompt.shape
    n_dec = x_sample.shape[0]
    n_pages = page_table.shape[1]
    l_all = seq + N_META_TOK
    nq = seq // TM
    layer = 0

    w = w_in[layer]
    win = jnp.concatenate(
        [w[:, 0:512], w[:, 544:4128], w[:, 512:544], jnp.zeros((d, C_END - C_KR - ROPE), F32)],
        axis=1).astype(BF16)
    wuq = jnp.pad(w_uq[layer].reshape(Q_RANK, HEADS, NOPE + ROPE),
                  ((0, 0), (0, 0), (0, SLAB - NOPE - ROPE))).reshape(Q_RANK, HEADS * SLAB).astype(BF16)
    wuk = jnp.pad(w_uk[layer], ((0, 0), (0, 0), (0, SLAB - NOPE))).reshape(KV_RANK, HEADS * SLAB)
    wuv = w_uv[layer].reshape(KV_RANK, HEADS * MLA_V)
    wkv = jnp.concatenate([wuk, wuv], axis=1).astype(BF16)
    wuv_b = wuv.astype(BF16)
    row = lambda v: v.reshape(1, -1)
    lng, lnb = row(ln_in_g), row(ln_in_b)
    bg, qn, kvn = row(b_gate[layer]), row(mla_q_norm[layer]), row(mla_kv_norm[layer])
    subln = row(jnp.tile(diff_subln[layer], LANES // DIFF_V))
    dl = diff_lambda[layer].astype(F32)
    front_params = (lng, lnb, win, bg, qn, wuq, kvn, wkv)
    front_param_specs = [_const_spec(p.shape) for p in front_params]

    cos_p, sin_p = _rope_tables(jnp.arange(N_META_TOK, l_all))
    cos_m, sin_m = _rope_tables(jnp.arange(LANES))
    cos_s, sin_s = _rope_tables(jnp.full((n_dec,), n_pages * PAGE))

    tok = lambda wdt, dt: jax.ShapeDtypeStruct((n_b, seq, wdt), dt)
    cache_shape = lambda wdt: jax.ShapeDtypeStruct((DEPTH, n_b, l_all, wdt), F32)
    tok_spec = lambda wdt: pl.BlockSpec((None, TM, wdt), lambda b, i: (b, i, 0))
    cache_spec = lambda wdt: pl.BlockSpec((pl.Element(1), pl.Element(1), pl.Element(TM), pl.Element(wdt)),
                                          lambda b, i: (0, b, pl.multiple_of(N_META_TOK + i * TM, N_META_TOK), 0))
    (lat_p, kr_p, kd_p, vd_p, q_mla, k_mla, v_mla, q_diff, k_diff, v_diff, gates_p) = pl.pallas_call(
        _front_prompt_kernel,
        grid=(n_b, nq),
        in_specs=[tok_spec(d), pl.BlockSpec((TM, LANES), lambda b, i: (i, 0)),
                  pl.BlockSpec((TM, LANES), lambda b, i: (i, 0))] + front_param_specs,
        out_specs=[cache_spec(KV_RANK), cache_spec(ROPE), cache_spec(512), cache_spec(512),
                   tok_spec(HEADS * SLAB), tok_spec(HEADS * SLAB), tok_spec(512),
                   tok_spec(512), tok_spec(512), tok_spec(512), tok_spec(2 * d)],
        out_shape=[cache_shape(KV_RANK), cache_shape(ROPE), cache_shape(512), cache_shape(512),
                   tok(HEADS * SLAB, BF16), tok(HEADS * SLAB, BF16), tok(512, BF16),
                   tok(512, BF16), tok(512, BF16), tok(512, BF16), tok(2 * d, F32)],
        compiler_params=_cparams(("arbitrary", "arbitrary"), 48),
        name="front_prompt",
    )(x_prompt, cos_p, sin_p, *front_params)

    meta_pad = jnp.pad(meta_tokens.astype(F32), ((0, LANES - N_META_TOK), (0, 0)))
    any_spec = pl.BlockSpec(memory_space=pl.ANY)
    meta_cache_spec = lambda wdt: pl.BlockSpec((None, None, N_META_TOK, wdt), lambda b: (0, b, 0, 0))
    mrow = lambda wdt: jax.ShapeDtypeStruct((LANES, wdt), BF16)
    mrow_spec = lambda wdt: pl.BlockSpec((LANES, wdt), lambda b: (0, 0))
    n_fp = 3 + len(front_params)
    (lat_p, kr_p, kd_p, vd_p, km_meta, vm_meta, kd_meta, vd_meta) = pl.pallas_call(
        _front_meta_kernel,
        grid=(n_b,),
        in_specs=[_const_spec((LANES, d)), _const_spec((LANES, LANES)), _const_spec((LANES, LANES))]
        + front_param_specs + [any_spec] * 4,
        out_specs=[meta_cache_spec(KV_RANK), meta_cache_spec(ROPE), meta_cache_spec(512), meta_cache_spec(512),
                   mrow_spec(HEADS * SLAB), mrow_spec(512), mrow_spec(512), mrow_spec(512)],
        out_shape=[cache_shape(KV_RANK), cache_shape(ROPE), cache_shape(512), cache_shape(512),
                   mrow(HEADS * SLAB), mrow(512), mrow(512), mrow(512)],
        input_output_aliases={n_fp: 0, n_fp + 1: 1, n_fp + 2: 2, n_fp + 3: 3},
        compiler_params=_cparams(("arbitrary",), 40),
        name="front_meta",
    )(meta_pad, cos_m, sin_m, *front_params, lat_p, kr_p, kd_p, vd_p)

    xs = x_sample.reshape(n_dec, d)
    sds = lambda shape, dt: jax.ShapeDtypeStruct(shape, dt)
    full = lambda shape: pl.BlockSpec(shape, lambda i: (0,) * len(shape))
    s_shapes = [(n_dec, KV_RANK), (n_dec, ROPE), (n_dec, 512), (n_dec, 512), (n_dec, 2 * d),
                (HEADS, n_dec, KV_RANK), (HEADS, n_dec, ROPE), (n_dec, 512), (DEC_ROWS, n_dec, LANES)]
    (lat_s, kr_s, kd_s, vd_s, gates_s, qlat_s, qrope_s, qd_s, ss_s) = pl.pallas_call(
        _front_sample_kernel,
        grid=(1,),
        in_specs=[_const_spec((n_dec, d)), _const_spec((n_dec, LANES)), _const_spec((n_dec, LANES))]
        + front_param_specs,
        out_specs=[full(s) for s in s_shapes],
        out_shape=[sds(s, F32) for s in s_shapes],
        compiler_params=_cparams(("arbitrary",), 40),
        name="front_sample",
    )(xs, cos_s, sin_s, *front_params)

    o_mla, o_diff = _prompt_attention(q_mla, k_mla, v_mla, km_meta, vm_meta,
                                      q_diff, k_diff, v_diff, kd_meta, vd_meta, dl, subln)

    kr_t = jnp.transpose(cache_mla_krope, (0, 1, 3, 2))
    kd_t = jnp.transpose(cache_diff_k, (0, 1, 3, 4, 2))
    vd_t = jnp.transpose(cache_diff_v, (0, 1, 3, 4, 2))
    qlat_b = jnp.transpose(qlat_s, (1, 0, 2))
    qrope_b = jnp.transpose(qrope_s, (1, 0, 2))
    ss_b = jnp.transpose(ss_s, (1, 0, 2))
    om_s, od_s = _decode_attention(page_table, qlat_b, qrope_b, qd_s, ss_b, lat_s,
                                   vd_s.reshape(n_dec, HEADS, DIFF_V), dl, wuv_b,
                                   cache_mla_latent, kr_t, kd_t, vd_t)

    back_params = (lng, lnb, subln, w_br_mla[layer].astype(BF16), w_br_diff[layer].astype(BF16),
                   w_out[layer].astype(BF16), row(ln1_g[layer]), row(ln1_b[layer]), w_up[layer].astype(BF16),
                   w_down[layer].astype(BF16), row(ln2_g[layer]), row(ln2_b[layer]))
    back_param_specs = [_const_spec(p.shape) for p in back_params]
    y_prompt = pl.pallas_call(
        functools.partial(_back_kernel, False),
        grid=(n_b, nq),
        in_specs=[tok_spec(d), tok_spec(512), tok_spec(512), tok_spec(2 * d)] + back_param_specs,
        out_specs=tok_spec(d),
        out_shape=jax.ShapeDtypeStruct((n_b, seq, d), F32),
        compiler_params=_cparams(("arbitrary", "arbitrary"), 56),
        name="back_prompt",
    )(x_prompt, o_mla, o_diff, gates_p, *back_params)

    y_sample = pl.pallas_call(
        functools.partial(_back_kernel, True),
        grid=(1,),
        in_specs=[full((n_dec, d)), full((n_dec, 512)), full((n_dec, 512)), full((n_dec, 2 * d))]
        + back_param_specs,
        out_specs=full((n_dec, d)),
        out_shape=jax.ShapeDtypeStruct((n_dec, d), F32),
        compiler_params=_cparams(("arbitrary",), 56),
        name="back_sample",
    )(xs, om_s.reshape(n_dec, 512), od_s.reshape(n_dec, 512), gates_s, *back_params)

    head_shape = lambda a, lead, v: a.reshape(lead + (HEADS, v))
    return (y_prompt, y_sample.reshape(n_dec, 1, d),
            lat_p, kr_p,
            head_shape(kd_p, (DEPTH, n_b, l_all), 2 * DIFF_DIM), head_shape(vd_p, (DEPTH, n_b, l_all), DIFF_V),
            lat_s.reshape(DEPTH, n_dec, 1, KV_RANK), kr_s.reshape(DEPTH, n_dec, 1, ROPE),
            head_shape(kd_s, (DEPTH, n_dec, 1), 2 * DIFF_DIM), head_shape(vd_s, (DEPTH, n_dec, 1), DIFF_V))
```

```python
import functools
import math

import jax
import jax.numpy as jnp
from jax import lax
from jax.experimental import pallas as pl
from jax.experimental.pallas import tpu as pltpu

F32 = jnp.float32
BF16 = jnp.bfloat16

D_MODEL = 1024
N_META_TOK = 16
PAGE = 128
HEADS = 8
Q_RANK = 256
KV_RANK = 256
NOPE = 64
ROPE = 32
MLA_V = 64
DIFF_DIM = 32
DIFF_V = 64
D_FF = 4 * D_MODEL
DEPTH = 1
ROPE_THETA = 10000.0
LN_EPS = 1e-5
RMS_EPS = 1e-6
LOG2E = math.log2(math.e)
MLA_SCALE = (NOPE + ROPE) ** -0.5 * LOG2E
DIFF_SCALE = DIFF_DIM ** -0.5 * LOG2E
DEEPNORM_ALPHA = (2 * DEPTH) ** 0.25
LAM_INIT = 0.8 - 0.6 * math.exp(-0.3 * 0)

LANES = 128
SUBLANES = 8
SLAB = 128
HALF = DIFF_DIM // 2

C_CQ, C_CKV, C_DQ, C_DK, C_DV, C_G, C_KR, C_END = 0, 256, 512, 1024, 1536, 2048, 4096, 4224

TM = 256
TQ_MLA, TK_MLA = 1024, 1024
TQ_DIFF, TK_DIFF = 512, 1024
DEC_G = 8
DEC_BUFS = 4
DEC_ROWS = 32
NEG_INF = float("-inf")


def _cparams(sem, vmem_mb):
    return pltpu.CompilerParams(dimension_semantics=sem, vmem_limit_bytes=vmem_mb * 1024 * 1024)


def _dot(a, b):
    return jnp.dot(a, b, preferred_element_type=F32)


def _dot_nt(a, b):
    return lax.dot_general(a, b, (((1,), (1,)), ((), ())), preferred_element_type=F32)


def _layer_norm(x, g, b):
    mu = jnp.mean(x, axis=-1, keepdims=True)
    xc = x - mu
    var = jnp.mean(xc * xc, axis=-1, keepdims=True)
    return xc * lax.rsqrt(var + LN_EPS) * g + b


def _rms_norm(x, g):
    return x * lax.rsqrt(jnp.mean(x * x, axis=-1, keepdims=True) + RMS_EPS) * g


def _rope_slabs(x, c, s_up, s_dn):
    outs = []
    for s in range(x.shape[1] // LANES):
        xs = x[:, LANES * s:LANES * (s + 1)]
        outs.append(xs * c + pltpu.roll(xs, HALF, 1) * s_up + pltpu.roll(xs, LANES - HALF, 1) * s_dn)
    return outs[0] if len(outs) == 1 else jnp.concatenate(outs, axis=1)


def _sub_ln_slab(x, g128):
    lane = lax.broadcasted_iota(jnp.int32, x.shape, 1)
    lo = lane < DIFF_V
    x2 = x * x
    s0 = jnp.sum(jnp.where(lo, x2, 0.0), axis=1, keepdims=True)
    s1 = jnp.sum(jnp.where(lo, 0.0, x2), axis=1, keepdims=True)
    ms = jnp.where(lo, s0, s1) * (1.0 / DIFF_V)
    return x * lax.rsqrt(ms + RMS_EPS) * g128 * (1.0 - LAM_INIT)


def _lambda(dl):
    a = jnp.sum(dl[0:1, :] * dl[1:2, :], axis=1, keepdims=True)
    b = jnp.sum(dl[2:3, :] * dl[3:4, :], axis=1, keepdims=True)
    return jnp.exp(a) - jnp.exp(b) + LAM_INIT


def _front_core(x, cosb, sinb, lng, lnb, win_ref, bg, qn, wuq_ref, kvn, wkv_ref):
    tm = x.shape[0]
    hb = _layer_norm(x, lng, lnb).astype(BF16)

    lane = lax.broadcasted_iota(jnp.int32, (tm, LANES), 1)
    first = (lane & (DIFF_DIM - 1)) < HALF
    zero = jnp.zeros_like(sinb)
    s_up_d = jnp.where(first, zero, sinb)
    s_dn_d = jnp.where(first, -sinb, zero)
    c_q = jnp.where(lane < NOPE, 1.0, cosb)
    s_up_q = jnp.where((lane >= NOPE + HALF) & (lane < NOPE + ROPE), sinb, zero)
    s_dn_q = jnp.where((lane >= NOPE) & (lane < NOPE + HALF), -sinb, zero)
    c_k = jnp.where(lane < ROPE, cosb, zero)
    s_up_k = jnp.where((lane >= HALF) & (lane < ROPE), sinb, zero)
    s_dn_k = jnp.where(lane < HALF, -sinb, zero)

    ya = _dot(hb, win_ref[:, C_CQ:C_DQ])
    cqn = _rms_norm(ya[:, :Q_RANK], qn).astype(BF16)
    q = _rope_slabs(_dot(cqn, wuq_ref[...]), c_q, s_up_q, s_dn_q) * MLA_SCALE
    lat = _rms_norm(ya[:, Q_RANK:], kvn)
    kv = _dot(lat.astype(BF16), wkv_ref[...])
    ykr = _dot(hb, win_ref[:, C_KR:C_END])
    kr = ykr * c_k + pltpu.roll(ykr, HALF, 1) * s_up_k + pltpu.roll(ykr, LANES - HALF, 1) * s_dn_k
    kr_at_rope = pltpu.roll(kr, NOPE, 1)
    k_mla = jnp.concatenate(
        [kv[:, SLAB * s:SLAB * (s + 1)] + kr_at_rope for s in range(HEADS)], axis=1)
    v_mla = kv[:, HEADS * SLAB:]

    yd = _dot(hb, win_ref[:, C_DQ:C_G])
    qd = _rope_slabs(yd[:, 0:512], cosb, s_up_d, s_dn_d) * DIFF_SCALE
    kd = _rope_slabs(yd[:, 512:1024], cosb, s_up_d, s_dn_d)
    vd = yd[:, 1024:1536]
    gates = jax.nn.sigmoid(_dot(hb, win_ref[:, C_G:C_KR]) + bg)
    return dict(q=q, lat=lat, kr=kr, k_mla=k_mla, v_mla=v_mla, qd=qd, kd=kd, vd=vd, gates=gates)


def _front_prompt_kernel(x_ref, cos_ref, sin_ref, lng_ref, lnb_ref, win_ref, bg_ref, qn_ref, wuq_ref,
                         kvn_ref, wkv_ref,
                         lat_o, kr_o, kd_o, vd_o, qm_o, km_o, vm_o, qd_o, kdb_o, vdb_o, g_o):
    r = _front_core(x_ref[...], cos_ref[...], sin_ref[...], lng_ref[...], lnb_ref[...], win_ref,
                    bg_ref[...], qn_ref[...], wuq_ref, kvn_ref[...], wkv_ref)
    lat_o[0, 0] = r["lat"]
    kr_o[0, 0] = r["kr"][:, :ROPE]
    kd_o[0, 0] = r["kd"]
    vd_o[0, 0] = r["vd"]
    qm_o[...] = r["q"].astype(BF16)
    km_o[...] = r["k_mla"].astype(BF16)
    vm_o[...] = r["v_mla"].astype(BF16)
    qd_o[...] = r["qd"].astype(BF16)
    kdb_o[...] = r["kd"].astype(BF16)
    vdb_o[...] = r["vd"].astype(BF16)
    g_o[...] = r["gates"]


def _front_meta_kernel(x_ref, cos_ref, sin_ref, lng_ref, lnb_ref, win_ref, bg_ref, qn_ref, wuq_ref,
                       kvn_ref, wkv_ref, lat_in, kr_in, kd_in, vd_in,
                       lat_o, kr_o, kd_o, vd_o, km_o, vm_o, kdb_o, vdb_o):
    del lat_in, kr_in, kd_in, vd_in
    r = _front_core(x_ref[...], cos_ref[...], sin_ref[...], lng_ref[...], lnb_ref[...], win_ref,
                    bg_ref[...], qn_ref[...], wuq_ref, kvn_ref[...], wkv_ref)
    n = N_META_TOK
    lat_o[...] = r["lat"][:n]
    kr_o[...] = r["kr"][:n, :ROPE]
    kd_o[...] = r["kd"][:n]
    vd_o[...] = r["vd"][:n]
    km_o[...] = r["k_mla"].astype(BF16)
    vm_o[...] = r["v_mla"].astype(BF16)
    kdb_o[...] = r["kd"].astype(BF16)
    vdb_o[...] = r["vd"].astype(BF16)


def _front_sample_kernel(x_ref, cos_ref, sin_ref, lng_ref, lnb_ref, win_ref, bg_ref, qn_ref, wuq_ref,
                         kvn_ref, wkv_ref,
                         lat_o, kr_o, kd_o, vd_o, g_o, qlat_o, qrope_o, qd_o, ss_o):
    r = _front_core(x_ref[...], cos_ref[...], sin_ref[...], lng_ref[...], lnb_ref[...], win_ref,
                    bg_ref[...], qn_ref[...], wuq_ref, kvn_ref[...], wkv_ref)
    nb = x_ref.shape[0]
    lat, kr, kd, q, qd = r["lat"], r["kr"], r["kd"], r["q"], r["qd"]
    lat_o[...] = lat
    kr_o[...] = kr[:, :ROPE]
    kd_o[...] = kd
    vd_o[...] = r["vd"]
    g_o[...] = r["gates"]
    qd_o[...] = qd

    lane = lax.broadcasted_iota(jnp.int32, (nb, LANES), 1)
    kr_at_rope = pltpu.roll(kr, NOPE, 1)
    rope_lanes = (lane >= NOPE) & (lane < NOPE + ROPE)
    rep = lambda col: jnp.broadcast_to(col, (nb, LANES))
    for h in range(HEADS):
        q_slab = q[:, SLAB * h:SLAB * (h + 1)]
        q_lat = _dot_nt(q_slab.astype(BF16), wkv_ref[:, SLAB * h:SLAB * (h + 1)])
        qlat_o[h] = q_lat
        qrope_o[h] = pltpu.roll(q_slab, LANES - NOPE, 1)[:, :ROPE]
        s_h = (jnp.sum(q_lat * lat, axis=1, keepdims=True)
               + jnp.sum(jnp.where(rope_lanes, q_slab * kr_at_rope, 0.0), axis=1, keepdims=True))
        ss_o[h] = rep(s_h)
    lane4 = lax.broadcasted_iota(jnp.int32, (nb, 4 * LANES), 1)
    prod = qd * kd
    for h in range(HEADS):
        for m in range(2):
            lo = h * 2 * DIFF_DIM + m * DIFF_DIM
            seg = (lane4 >= lo) & (lane4 < lo + DIFF_DIM)
            ss_o[HEADS + HEADS * m + h] = rep(jnp.sum(jnp.where(seg, prod, 0.0), axis=1, keepdims=True))
    for c in range(3 * HEADS, DEC_ROWS):
        ss_o[c] = jnp.zeros((nb, LANES), F32)


def _softmax_step(blocks, first, m_scr, l_scr, acc_scr):
    m_cur = None
    for s, _ in blocks:
        m_b = jnp.max(s, axis=1, keepdims=True)
        m_cur = m_b if m_cur is None else jnp.maximum(m_cur, m_b)
    if first:
        m_new = jnp.broadcast_to(m_cur, m_scr.shape)
    else:
        m_prev = m_scr[...]
        m_new = jnp.maximum(m_prev, m_cur)
        corr = jnp.exp2(m_prev - m_new)
    l_cur = None
    pv = None
    for s, v in blocks:
        reps = s.shape[1] // LANES
        p = jnp.exp2(s - (jnp.tile(m_new, (1, reps)) if reps > 1 else m_new))
        l_b = jnp.sum(p, axis=1, keepdims=True)
        pv_b = _dot(p.astype(BF16), v)
        l_cur = l_b if l_cur is None else l_cur + l_b
        pv = pv_b if pv is None else pv + pv_b
    if first:
        l_scr[...] = jnp.broadcast_to(l_cur, l_scr.shape)
        acc_scr[...] = pv
    else:
        l_scr[...] = corr * l_scr[...] + l_cur
        acc_scr[...] = corr * acc_scr[...] + pv
    m_scr[...] = m_new


def _scores(groups, k_at):
    parts = [_dot_nt(q, k_at(ko)) for q, ko in groups]
    return parts[0] if len(parts) == 1 else jnp.concatenate(parts, axis=0)


def _edge_step(qi, tq, tk, groups, k_ref, v_ref, km_ref, vm_ref, m_scr, l_scr, acc_scr):
    rows = sum(q.shape[0] for q, _ in groups)
    ratio = tk // tq

    def edge_blocks(n_rem):
        col = lax.broadcasted_iota(jnp.int32, (rows, LANES), 1)
        s = _scores(groups, lambda ko: km_ref[:, ko:ko + LANES])
        blocks = [(jnp.where(col < N_META_TOK, s, NEG_INF), vm_ref[...])]
        for j in range(n_rem):
            start = pl.multiple_of((qi - n_rem + j) * tq, tq)
            blocks.append((_scores(groups, lambda ko: k_ref[pl.ds(start, tq), ko:ko + LANES]),
                           v_ref[pl.ds(start, tq), :]))
        start = pl.multiple_of(qi * tq, tq)
        row = lax.broadcasted_iota(jnp.int32, (rows, tq), 0)
        colq = lax.broadcasted_iota(jnp.int32, (rows, tq), 1)
        s = _scores(groups, lambda ko: k_ref[pl.ds(start, tq), ko:ko + LANES])
        blocks.append((jnp.where(colq <= (row & (tq - 1)), s, NEG_INF), v_ref[pl.ds(start, tq), :]))
        return blocks

    if ratio == 1:
        _softmax_step(edge_blocks(0), True, m_scr, l_scr, acc_scr)
    else:
        for n_rem in range(ratio):
            @pl.when(qi % ratio == n_rem)
            def _():
                _softmax_step(edge_blocks(n_rem), True, m_scr, l_scr, acc_scr)


def _full_step(c, tk, groups, k_ref, v_ref, m_scr, l_scr, acc_scr):
    start = pl.multiple_of(c * tk, tk)
    s = _scores(groups, lambda ko: k_ref[pl.ds(start, tk), ko:ko + LANES])
    _softmax_step([(s, v_ref[pl.ds(start, tk), :])], False, m_scr, l_scr, acc_scr)


def _causal_sweep(qi, tq, tk, groups, k_ref, v_ref, km_ref, vm_ref, m_scr, l_scr, acc_scr):
    _edge_step(qi, tq, tk, groups, k_ref, v_ref, km_ref, vm_ref, m_scr, l_scr, acc_scr)

    def full_chunk(c, carry):
        _full_step(c, tk, groups, k_ref, v_ref, m_scr, l_scr, acc_scr)
        return carry

    lax.fori_loop(0, qi // (tk // tq), full_chunk, 0)


def _attn_mla_kernel(q_ref, k_ref, v_ref, km_ref, vm_ref, o_ref, m_scr, l_scr, acc_scr):
    qi = pl.program_id(2)
    tq = q_ref.shape[0]
    groups = [(q_ref[:, SLAB * hh:SLAB * (hh + 1)], SLAB * hh) for hh in range(2)]
    _causal_sweep(qi, tq, TK_MLA, groups, k_ref, v_ref, km_ref, vm_ref, m_scr, l_scr, acc_scr)
    lane = lax.broadcasted_iota(jnp.int32, (tq, LANES), 1)
    o = acc_scr[...] / l_scr[...]
    o_ref[...] = jnp.where(lane < MLA_V, o[0:tq], o[tq:2 * tq]).astype(o_ref.dtype)


def _attn_diff_kernel(q_ref, k_ref, v_ref, km_ref, vm_ref, dl_ref, g_ref, o_ref, qs_scr, m_scr, l_scr, acc_scr):
    qi = pl.program_id(2)
    tq = q_ref.shape[0]
    q = q_ref[...]
    lane = lax.broadcasted_iota(jnp.int32, (tq, LANES), 1)
    zero = jnp.zeros_like(q)
    for j in range(4):
        qs_scr[j * tq:(j + 1) * tq, :] = jnp.where((lane >= DIFF_DIM * j) & (lane < DIFF_DIM * (j + 1)), q, zero)
    _causal_sweep(qi, tq, TK_DIFF, [(qs_scr[...], 0)], k_ref, v_ref, km_ref, vm_ref, m_scr, l_scr, acc_scr)
    lam = _lambda(dl_ref[...])
    o = acc_scr[...] / l_scr[...]
    o0 = o[0:tq] - lam * o[tq:2 * tq]
    o1 = o[2 * tq:3 * tq] - lam * o[3 * tq:4 * tq]
    o_ref[...] = _sub_ln_slab(jnp.where(lane < DIFF_V, o0, o1), g_ref[...]).astype(o_ref.dtype)


def _decode_step(step, n_steps, n_t, pt_ref, qlat_ref, qrope_ref, qd_ref, ss_ref, lats_ref, vds_ref, dl_ref,
                 wuv_ref, lat_hbm, kr_hbm, kt_hbm, vt_hbm, om_ref, od_ref,
                 lat_buf, kr_buf, kt_buf, vt_buf, sem, qcol, ml, ll, accl, md, ld, accd):
    g_n = DEC_G
    b = step // n_t
    t = lax.rem(step, n_t)
    r = lax.rem(b, SUBLANES)
    n_maps = 2 * HEADS

    def page_copies(src_step, slot_, g):
        page = pt_ref[src_step * g_n + g]
        return (pltpu.make_async_copy(lat_hbm.at[0, page], lat_buf.at[slot_, g], sem.at[slot_, 0]),
                pltpu.make_async_copy(kr_hbm.at[0, page], kr_buf.at[slot_, g], sem.at[slot_, 1]),
                pltpu.make_async_copy(kt_hbm.at[0, page], kt_buf.at[slot_, g], sem.at[slot_, 2]),
                pltpu.make_async_copy(vt_hbm.at[0, page], vt_buf.at[slot_, g], sem.at[slot_, 3]))

    ahead = DEC_BUFS - 1
    slot = lax.rem(step, DEC_BUFS)
    nxt_step = lax.rem(step + ahead, n_steps)
    nxt_slot = lax.rem(step + ahead, DEC_BUFS)

    @pl.when(step == 0)
    def _():
        for s0 in range(ahead):
            for g in range(g_n):
                for c in page_copies(s0, s0, g):
                    c.start()

    for g in range(g_n):
        for c in page_copies(step, slot, g):
            c.wait()
    for g in range(g_n):
        for c in page_copies(nxt_step, nxt_slot, g):
            c.start()

    lat_refs = [lat_buf.at[slot, g] for g in range(g_n)]
    kr_refs = [kr_buf.at[slot, g] for g in range(g_n)]
    kt_refs = [kt_buf.at[slot, g] for g in range(g_n)]
    vt_refs = [vt_buf.at[slot, g] for g in range(g_n)]

    @pl.when(t == 0)
    def _():
        qrow = qd_ref[pl.ds(r, 1), :]
        for s in range(HEADS // 2):
            cols = jnp.broadcast_to(qrow[:, LANES * s:LANES * (s + 1)], (LANES, LANES)).T
            qcol[2 * s] = cols[0:2 * DIFF_DIM]
            qcol[2 * s + 1] = cols[2 * DIFF_DIM:]
        ss = ss_ref[...]
        ml[...] = ss[0:HEADS]
        md[...] = ss[HEADS:HEADS + n_maps]
        ll[...] = jnp.zeros(ll.shape, F32)
        ld[...] = jnp.zeros(ld.shape, F32)
        accl[...] = jnp.zeros(accl.shape, F32)
        accd[...] = jnp.zeros(accd.shape, F32)

    qlat = qlat_ref[...].astype(BF16)
    qrope = qrope_ref[...].astype(BF16)
    lat_b = [ref[...].astype(BF16) for ref in lat_refs]
    s_l = [_dot_nt(qlat, lat_b[g]) + _dot(qrope, kr_refs[g][...].astype(BF16)) for g in range(g_n)]
    mx = s_l[0]
    for g in range(1, g_n):
        mx = jnp.maximum(mx, s_l[g])
    m_old = ml[...]
    m_new = jnp.maximum(m_old, jnp.max(mx, axis=1, keepdims=True))
    corr = jnp.exp2(m_old - m_new)
    psum = jnp.zeros((HEADS, LANES), F32)
    o_l = jnp.zeros((HEADS, KV_RANK), F32)
    for g in range(g_n):
        p = jnp.exp2(s_l[g] - m_new)
        psum = psum + p
        o_l = o_l + _dot(p.astype(BF16), lat_b[g])
    ll[...] = ll[...] * corr + jnp.sum(psum, axis=1, keepdims=True)
    accl[...] = accl[...] * jnp.tile(corr, (1, KV_RANK // LANES)) + o_l
    ml[...] = m_new

    for h in range(HEADS):
        qc = qcol[h]
        rows = ([], [])
        for g in range(g_n):
            prod = kt_refs[g][h] * qc
            rows[0].append(jnp.sum(prod[0:DIFF_DIM], axis=0, keepdims=True))
            rows[1].append(jnp.sum(prod[DIFF_DIM:], axis=0, keepdims=True))
        for m in range(2):
            idx = HEADS * m + h
            s_d = jnp.concatenate(rows[m], axis=0)
            m_old = md[idx:idx + 1, :]
            m_new = jnp.maximum(
                m_old, jnp.max(jnp.max(s_d, axis=1, keepdims=True), axis=0, keepdims=True))
            corr = jnp.exp2(m_old - m_new)
            p_d = jnp.exp2(s_d - m_new)
            ld[idx:idx + 1, :] = ld[idx:idx + 1, :] * corr + jnp.sum(
                jnp.sum(p_d, axis=1, keepdims=True), axis=0, keepdims=True)
            acc = accd[idx] * corr
            for g in range(g_n):
                acc = acc + vt_refs[g][h] * p_d[g:g + 1, :]
            accd[idx] = acc
            md[idx:idx + 1, :] = m_new

    @pl.when(t == n_t - 1)
    def _():
        ss = ss_ref[...]
        w_self = jnp.exp2(ss[0:HEADS] - ml[...])
        l_tot = ll[...] + w_self
        reps = KV_RANK // LANES
        o_lat = (accl[...] + jnp.tile(w_self, (1, reps)) * lats_ref[pl.ds(r, 1), :]) / jnp.tile(l_tot, (1, reps))
        om_full = _dot(o_lat.astype(BF16), wuv_ref[...])
        row8 = lax.broadcasted_iota(jnp.int32, (HEADS, MLA_V), 0)
        om = jnp.zeros((HEADS, MLA_V), F32)
        for h in range(HEADS):
            om = jnp.where(row8 == h, om_full[:, MLA_V * h:MLA_V * (h + 1)], om)
        om_ref[...] = om

        lane = lax.broadcasted_iota(jnp.int32, (DIFF_V, LANES), 1)
        cols = jnp.zeros((DIFF_V, LANES), F32)
        for idx in range(n_maps):
            cols = jnp.where(lane == idx, jnp.sum(accd[idx], axis=1, keepdims=True), cols)
        o_rows = jnp.concatenate([cols, jnp.zeros((LANES - DIFF_V, LANES), F32)], axis=0).T
        w_d = jnp.exp2(ss[HEADS:HEADS + n_maps] - md[...])
        l_d = ld[...] + w_d
        vd_self = vds_ref[...]
        o1 = (o_rows[0:HEADS, 0:DIFF_V] + w_d[0:HEADS, 0:DIFF_V] * vd_self) / l_d[0:HEADS, 0:DIFF_V]
        o2 = (o_rows[HEADS:n_maps, 0:DIFF_V] + w_d[HEADS:, 0:DIFF_V] * vd_self) / l_d[HEADS:, 0:DIFF_V]
        od_ref[...] = o1 - _lambda(dl_ref[...]) * o2

    @pl.when(step == n_steps - 1)
    def _():
        for k in range(ahead):
            for g in range(g_n):
                for c in page_copies(k, (n_steps + k) % DEC_BUFS, g):
                    c.wait()


N_DEC_IN, N_BACK_IN, N_DEC_SCRATCH = 12, 16, 12


def _decode_kernel(n_steps, n_t, pt_ref, *refs):
    _decode_step(pl.program_id(0) * n_t + pl.program_id(1), n_steps, n_t, pt_ref, *refs)


def _decode_back_kernel(n_steps, n_t, pt_ref, *refs):
    dec_in = refs[:N_DEC_IN]
    back_in = refs[N_DEC_IN:N_DEC_IN + N_BACK_IN]
    om_ref, od_ref, y_ref = refs[N_DEC_IN + N_BACK_IN:N_DEC_IN + N_BACK_IN + 3]
    scratch = refs[N_DEC_IN + N_BACK_IN + 3:]
    t = pl.program_id(1)
    _decode_step(pl.program_id(0) * n_t + t, n_steps, n_t, pt_ref, *dec_in, om_ref, od_ref, *scratch)

    @pl.when(t == n_t // 2)
    def _():
        _back_kernel(False, *back_in, y_ref)


def _back_kernel(raw_diff, x_ref, om_ref, od_ref, g_ref, lng_ref, lnb_ref, subln_ref, wbm_ref, wbd_ref,
                 wout_ref, ln1g_ref, ln1b_ref, wup_ref, wdn_ref, ln2g_ref, ln2b_ref, y_ref):
    h = _layer_norm(x_ref[...], lng_ref[...], lnb_ref[...])
    od = od_ref[...]
    if raw_diff:
        g128 = subln_ref[...]
        od = jnp.concatenate(
            [_sub_ln_slab(od[:, LANES * s:LANES * (s + 1)], g128) for s in range(od.shape[1] // LANES)], axis=1)
    a = _dot(om_ref[...].astype(BF16), wbm_ref[...])
    b = _dot(od.astype(BF16), wbd_ref[...])
    g = g_ref[...]
    mix = _dot((g[:, :D_MODEL] * a + g[:, D_MODEL:] * b).astype(BF16), wout_ref[...])
    x1 = _layer_norm(DEEPNORM_ALPHA * h + mix, ln1g_ref[...], ln1b_ref[...])
    up = jnp.maximum(_dot(x1.astype(BF16), wup_ref[...]), 0.0)
    f = _dot((up * up).astype(BF16), wdn_ref[...])
    y_ref[...] = _layer_norm(DEEPNORM_ALPHA * x1 + f, ln2g_ref[...], ln2b_ref[...])


def _const_spec(shape):
    nd = len(shape)
    return pl.BlockSpec(shape, lambda *_: (0,) * nd, pipeline_mode=pl.Buffered(1))


def _prompt_attention(q_mla, k_mla, v_mla, km_meta, vm_meta, q_diff, k_diff, v_diff, kd_meta, vd_meta, dl, subln):
    n_b, seq, _ = q_mla.shape
    sem = ("arbitrary", "arbitrary", "arbitrary")
    q_spec = lambda tq, wdt: pl.BlockSpec((None, tq, wdt), lambda b, p, i: (b, i, p))
    kv_spec = lambda wdt: pl.BlockSpec((None, seq, wdt), lambda b, p, i: (b, 0, p))
    meta_spec = lambda wdt: pl.BlockSpec((LANES, wdt), lambda b, p, i: (0, p))
    o_shape = jax.ShapeDtypeStruct((n_b, seq, HEADS * MLA_V), BF16)
    state = lambda rows: [pltpu.VMEM((rows, LANES), F32) for _ in range(3)]
    tq_m, tq_d = min(TQ_MLA, seq), min(TQ_DIFF, seq)

    o_mla = pl.pallas_call(
        _attn_mla_kernel,
        grid=(n_b, HEADS // 2, seq // tq_m),
        in_specs=[q_spec(tq_m, 2 * SLAB), kv_spec(2 * SLAB), kv_spec(LANES), meta_spec(2 * SLAB), meta_spec(LANES)],
        out_specs=q_spec(tq_m, LANES),
        out_shape=o_shape,
        scratch_shapes=state(2 * tq_m),
        compiler_params=_cparams(sem, 48),
        name="attn_mla",
    )(q_mla, k_mla, v_mla, km_meta, vm_meta)

    o_diff = pl.pallas_call(
        _attn_diff_kernel,
        grid=(n_b, HEADS // 2, seq // tq_d),
        in_specs=[q_spec(tq_d, LANES), kv_spec(LANES), kv_spec(LANES), meta_spec(LANES), meta_spec(LANES),
                  pl.BlockSpec(dl.shape, lambda b, p, i: (0, 0)), pl.BlockSpec(subln.shape, lambda b, p, i: (0, 0))],
        out_specs=q_spec(tq_d, LANES),
        out_shape=o_shape,
        scratch_shapes=[pltpu.VMEM((4 * tq_d, LANES), BF16)] + state(4 * tq_d),
        compiler_params=_cparams(sem, 48),
        name="attn_diff",
    )(q_diff, k_diff, v_diff, kd_meta, vd_meta, dl, subln)
    return o_mla, o_diff


def _back_rows_per_sequence(n_b, seq, n_dec):
    rows = (n_b * seq) // n_dec
    ok = (n_b * seq) % n_dec == 0 and rows >= LANES and rows % LANES == 0 and seq % rows == 0
    return rows if ok else 0


def _decode_attention(page_table, qlat_b, qrope_b, qd_s, ss_b, lat_s, vds, dl, wuv_b, cache_lat, kr_t, kd_t, vd_t,
                      back=None):
    n_dec, n_pages = page_table.shape
    n_t = n_pages // DEC_G
    pt_flat = page_table.reshape(-1).astype(jnp.int32)
    per_seq = lambda rows, wdt: pl.BlockSpec((None, rows, wdt), lambda b, t, pt: (b, 0, 0))
    group8 = lambda wdt: pl.BlockSpec((SUBLANES, wdt), lambda b, t, pt: (b // SUBLANES, 0))
    const2 = lambda a: pl.BlockSpec(a.shape, lambda b, t, pt: (0, 0))
    any_spec = pl.BlockSpec(memory_space=pl.ANY)
    dec_out_spec = pl.BlockSpec((None, HEADS, MLA_V), lambda b, t, pt: (b, 0, 0))
    body, back_args, back_specs = _decode_kernel, (), []
    out_specs = [dec_out_spec, dec_out_spec]
    out_shape = [jax.ShapeDtypeStruct((n_dec, HEADS, MLA_V), F32)] * 2
    if back is not None:
        x, o_mla, o_diff, gates, params = back
        n_b, seq, d = x.shape
        rows = _back_rows_per_sequence(n_b, seq, n_dec)
        per = seq // rows
        tile = lambda wdt: pl.BlockSpec((None, rows, wdt), lambda b, t, pt: (b // per, b % per, 0))
        body = _decode_back_kernel
        back_args = (x, o_mla, o_diff, gates) + tuple(params)
        back_specs = [tile(d), tile(o_mla.shape[2]), tile(o_diff.shape[2]), tile(gates.shape[2])] + [
            _const_spec(p.shape) for p in params]
        out_specs = out_specs + [tile(d)]
        out_shape = out_shape + [jax.ShapeDtypeStruct((n_b, seq, d), F32)]
    return pl.pallas_call(
        functools.partial(body, n_dec * n_t, n_t),
        grid_spec=pltpu.PrefetchScalarGridSpec(
            num_scalar_prefetch=1,
            grid=(n_dec, n_t),
            in_specs=[per_seq(HEADS, KV_RANK), per_seq(HEADS, ROPE), group8(512), per_seq(DEC_ROWS, LANES),
                      group8(KV_RANK), per_seq(HEADS, DIFF_V), const2(dl), const2(wuv_b)]
            + [any_spec] * 4 + back_specs,
            out_specs=out_specs,
            scratch_shapes=[pltpu.VMEM((DEC_BUFS, DEC_G, PAGE, KV_RANK), F32),
                            pltpu.VMEM((DEC_BUFS, DEC_G, ROPE, PAGE), F32),
                            pltpu.VMEM((DEC_BUFS, DEC_G, HEADS, 2 * DIFF_DIM, PAGE), F32),
                            pltpu.VMEM((DEC_BUFS, DEC_G, HEADS, DIFF_V, PAGE), F32),
                            pltpu.SemaphoreType.DMA((DEC_BUFS, 4)),
                            pltpu.VMEM((HEADS, 2 * DIFF_DIM, LANES), F32),
                            pltpu.VMEM((HEADS, LANES), F32), pltpu.VMEM((HEADS, LANES), F32),
                            pltpu.VMEM((HEADS, KV_RANK), F32),
                            pltpu.VMEM((2 * HEADS, LANES), F32), pltpu.VMEM((2 * HEADS, LANES), F32),
                            pltpu.VMEM((2 * HEADS, DIFF_V, LANES), F32)]),
        out_shape=out_shape,
        compiler_params=_cparams(("arbitrary", "arbitrary"), 57),
        name="decode_attn",
    )(pt_flat, qlat_b, qrope_b, qd_s, ss_b, lat_s, vds, dl, wuv_b, cache_lat, kr_t, kd_t, vd_t, *back_args)


def _rope_tables(pos):
    inv = ROPE_THETA ** (-jnp.arange(HALF, dtype=F32) / HALF)
    ang = pos.astype(F32)[:, None] * inv[None, :]
    reps = LANES // HALF
    return jnp.tile(jnp.cos(ang), (1, reps)), jnp.tile(jnp.sin(ang), (1, reps))


def kernel(x_prompt, x_sample, cache_mla_latent, cache_mla_krope, cache_diff_k, cache_diff_v, page_table,
           meta_tokens, ln_in_g, ln_in_b, w_in, b_gate, mla_q_norm, w_uq, mla_kv_norm, w_uk, w_uv,
           diff_lambda, diff_subln, w_br_mla, w_br_diff, w_out, ln1_g, ln1_b, w_up, w_down, ln2_g, ln2_b):
    n_b, seq, d = x_prompt.shape
    n_dec = x_sample.shape[0]
    n_pages = page_table.shape[1]
    l_all = seq + N_META_TOK
    nq = seq // TM
    layer = 0

    w = w_in[layer]
    win = jnp.concatenate(
        [w[:, 0:512], w[:, 544:4128], w[:, 512:544], jnp.zeros((d, C_END - C_KR - ROPE), F32)],
        axis=1).astype(BF16)
    wuq = jnp.pad(w_uq[layer].reshape(Q_RANK, HEADS, NOPE + ROPE),
                  ((0, 0), (0, 0), (0, SLAB - NOPE - ROPE))).reshape(Q_RANK, HEADS * SLAB).astype(BF16)
    wuk = jnp.pad(w_uk[layer], ((0, 0), (0, 0), (0, SLAB - NOPE))).reshape(KV_RANK, HEADS * SLAB)
    wuv = w_uv[layer].reshape(KV_RANK, HEADS * MLA_V)
    wkv = jnp.concatenate([wuk, wuv], axis=1).astype(BF16)
    wuv_b = wuv.astype(BF16)
    row = lambda v: v.reshape(1, -1)
    lng, lnb = row(ln_in_g), row(ln_in_b)
    bg, qn, kvn = row(b_gate[layer]), row(mla_q_norm[layer]), row(mla_kv_norm[layer])
    subln = row(jnp.tile(diff_subln[layer], LANES // DIFF_V))
    dl = diff_lambda[layer].astype(F32)
    front_params = (lng, lnb, win, bg, qn, wuq, kvn, wkv)
    front_param_specs = [_const_spec(p.shape) for p in front_params]

    cos_p, sin_p = _rope_tables(jnp.arange(N_META_TOK, l_all))
    cos_m, sin_m = _rope_tables(jnp.arange(LANES))
    cos_s, sin_s = _rope_tables(jnp.full((n_dec,), n_pages * PAGE))

    tok = lambda wdt, dt: jax.ShapeDtypeStruct((n_b, seq, wdt), dt)
    cache_shape = lambda wdt: jax.ShapeDtypeStruct((DEPTH, n_b, l_all, wdt), F32)
    tok_spec = lambda wdt: pl.BlockSpec((None, TM, wdt), lambda b, i: (b, i, 0))
    cache_spec = lambda wdt: pl.BlockSpec((pl.Element(1), pl.Element(1), pl.Element(TM), pl.Element(wdt)),
                                          lambda b, i: (0, b, pl.multiple_of(N_META_TOK + i * TM, N_META_TOK), 0))
    (lat_p, kr_p, kd_p, vd_p, q_mla, k_mla, v_mla, q_diff, k_diff, v_diff, gates_p) = pl.pallas_call(
        _front_prompt_kernel,
        grid=(n_b, nq),
        in_specs=[tok_spec(d), pl.BlockSpec((TM, LANES), lambda b, i: (i, 0)),
                  pl.BlockSpec((TM, LANES), lambda b, i: (i, 0))] + front_param_specs,
        out_specs=[cache_spec(KV_RANK), cache_spec(ROPE), cache_spec(512), cache_spec(512),
                   tok_spec(HEADS * SLAB), tok_spec(HEADS * SLAB), tok_spec(512),
                   tok_spec(512), tok_spec(512), tok_spec(512), tok_spec(2 * d)],
        out_shape=[cache_shape(KV_RANK), cache_shape(ROPE), cache_shape(512), cache_shape(512),
                   tok(HEADS * SLAB, BF16), tok(HEADS * SLAB, BF16), tok(512, BF16),
                   tok(512, BF16), tok(512, BF16), tok(512, BF16), tok(2 * d, F32)],
        compiler_params=_cparams(("arbitrary", "arbitrary"), 48),
        name="front_prompt",
    )(x_prompt, cos_p, sin_p, *front_params)

    meta_pad = jnp.pad(meta_tokens.astype(F32), ((0, LANES - N_META_TOK), (0, 0)))
    any_spec = pl.BlockSpec(memory_space=pl.ANY)
    meta_cache_spec = lambda wdt: pl.BlockSpec((None, None, N_META_TOK, wdt), lambda b: (0, b, 0, 0))
    mrow = lambda wdt: jax.ShapeDtypeStruct((LANES, wdt), BF16)
    mrow_spec = lambda wdt: pl.BlockSpec((LANES, wdt), lambda b: (0, 0))
    n_fp = 3 + len(front_params)
    (lat_p, kr_p, kd_p, vd_p, km_meta, vm_meta, kd_meta, vd_meta) = pl.pallas_call(
        _front_meta_kernel,
        grid=(n_b,),
        in_specs=[_const_spec((LANES, d)), _const_spec((LANES, LANES)), _const_spec((LANES, LANES))]
        + front_param_specs + [any_spec] * 4,
        out_specs=[meta_cache_spec(KV_RANK), meta_cache_spec(ROPE), meta_cache_spec(512), meta_cache_spec(512),
                   mrow_spec(HEADS * SLAB), mrow_spec(512), mrow_spec(512), mrow_spec(512)],
        out_shape=[cache_shape(KV_RANK), cache_shape(ROPE), cache_shape(512), cache_shape(512),
                   mrow(HEADS * SLAB), mrow(512), mrow(512), mrow(512)],
        input_output_aliases={n_fp: 0, n_fp + 1: 1, n_fp + 2: 2, n_fp + 3: 3},
        compiler_params=_cparams(("arbitrary",), 40),
        name="front_meta",
    )(meta_pad, cos_m, sin_m, *front_params, lat_p, kr_p, kd_p, vd_p)

    xs = x_sample.reshape(n_dec, d)
    sds = lambda shape, dt: jax.ShapeDtypeStruct(shape, dt)
    full = lambda shape: pl.BlockSpec(shape, lambda i: (0,) * len(shape))
    s_shapes = [(n_dec, KV_RANK), (n_dec, ROPE), (n_dec, 512), (n_dec, 512), (n_dec, 2 * d),
                (HEADS, n_dec, KV_RANK), (HEADS, n_dec, ROPE), (n_dec, 512), (DEC_ROWS, n_dec, LANES)]
    (lat_s, kr_s, kd_s, vd_s, gates_s, qlat_s, qrope_s, qd_s, ss_s) = pl.pallas_call(
        _front_sample_kernel,
        grid=(1,),
        in_specs=[_const_spec((n_dec, d)), _const_spec((n_dec, LANES)), _const_spec((n_dec, LANES))]
        + front_param_specs,
        out_specs=[full(s) for s in s_shapes],
        out_shape=[sds(s, F32) for s in s_shapes],
        compiler_params=_cparams(("arbitrary",), 40),
        name="front_sample",
    )(xs, cos_s, sin_s, *front_params)

    o_mla, o_diff = _prompt_attention(q_mla, k_mla, v_mla, km_meta, vm_meta,
                                      q_diff, k_diff, v_diff, kd_meta, vd_meta, dl, subln)

    kr_t = jnp.transpose(cache_mla_krope, (0, 1, 3, 2))
    kd_t = jnp.transpose(cache_diff_k, (0, 1, 3, 4, 2))
    vd_t = jnp.transpose(cache_diff_v, (0, 1, 3, 4, 2))
    qlat_b = jnp.transpose(qlat_s, (1, 0, 2))
    qrope_b = jnp.transpose(qrope_s, (1, 0, 2))
    ss_b = jnp.transpose(ss_s, (1, 0, 2))
    back_params = (lng, lnb, subln, w_br_mla[layer].astype(BF16), w_br_diff[layer].astype(BF16),
                   w_out[layer].astype(BF16), row(ln1_g[layer]), row(ln1_b[layer]), w_up[layer].astype(BF16),
                   w_down[layer].astype(BF16), row(ln2_g[layer]), row(ln2_b[layer]))
    back_param_specs = [_const_spec(p.shape) for p in back_params]
    dec_args = (page_table, qlat_b, qrope_b, qd_s, ss_b, lat_s, vd_s.reshape(n_dec, HEADS, DIFF_V), dl, wuv_b,
                cache_mla_latent, kr_t, kd_t, vd_t)
    if _back_rows_per_sequence(n_b, seq, n_dec):
        om_s, od_s, y_prompt = _decode_attention(*dec_args, back=(x_prompt, o_mla, o_diff, gates_p, back_params))
    else:
        om_s, od_s = _decode_attention(*dec_args)
        y_prompt = pl.pallas_call(
            functools.partial(_back_kernel, False),
            grid=(n_b, nq),
            in_specs=[tok_spec(d), tok_spec(512), tok_spec(512), tok_spec(2 * d)] + back_param_specs,
            out_specs=tok_spec(d),
            out_shape=jax.ShapeDtypeStruct((n_b, seq, d), F32),
            compiler_params=_cparams(("arbitrary", "arbitrary"), 56),
            name="back_prompt",
        )(x_prompt, o_mla, o_diff, gates_p, *back_params)

    y_sample = pl.pallas_call(
        functools.partial(_back_kernel, True),
        grid=(1,),
        in_specs=[full((n_dec, d)), full((n_dec, 512)), full((n_dec, 512)), full((n_dec, 2 * d))]
        + back_param_specs,
        out_specs=full((n_dec, d)),
        out_shape=jax.ShapeDtypeStruct((n_dec, d), F32),
        compiler_params=_cparams(("arbitrary",), 56),
        name="back_sample",
    )(xs, om_s.reshape(n_dec, 512), od_s.reshape(n_dec, 512), gates_s, *back_params)

    head_shape = lambda a, lead, v: a.reshape(lead + (HEADS, v))
    return (y_prompt, y_sample.reshape(n_dec, 1, d),
            lat_p, kr_p,
            head_shape(kd_p, (DEPTH, n_b, l_all), 2 * DIFF_DIM), head_shape(vd_p, (DEPTH, n_b, l_all), DIFF_V),
            lat_s.reshape(DEPTH, n_dec, 1, KV_RANK), kr_s.reshape(DEPTH, n_dec, 1, ROPE),
            head_shape(kd_s, (DEPTH, n_dec, 1), 2 * DIFF_DIM), head_shape(vd_s, (DEPTH, n_dec, 1), DIFF_V))
```

```python
import functools
import math

import jax
import jax.numpy as jnp
from jax import lax
from jax.experimental import pallas as pl
from jax.experimental.pallas import tpu as pltpu

F32 = jnp.float32
BF16 = jnp.bfloat16

D_MODEL = 1024
N_META_TOK = 16
PAGE = 128
HEADS = 8
Q_RANK = 256
KV_RANK = 256
NOPE = 64
ROPE = 32
MLA_V = 64
DIFF_DIM = 32
DIFF_V = 64
D_FF = 4 * D_MODEL
DEPTH = 1
ROPE_THETA = 10000.0
LN_EPS = 1e-5
RMS_EPS = 1e-6
LOG2E = math.log2(math.e)
MLA_SCALE = (NOPE + ROPE) ** -0.5 * LOG2E
DIFF_SCALE = DIFF_DIM ** -0.5 * LOG2E
DEEPNORM_ALPHA = (2 * DEPTH) ** 0.25
LAM_INIT = 0.8 - 0.6 * math.exp(-0.3 * 0)

LANES = 128
SUBLANES = 8
SLAB = 128
HALF = DIFF_DIM // 2

C_CQ, C_CKV, C_DQ, C_DK, C_DV, C_G, C_KR, C_END = 0, 256, 512, 1024, 1536, 2048, 4096, 4224

TM = 256
TQ_MLA, TK_MLA = 1024, 1024
TQ_DIFF, TK_DIFF = 512, 1024
DEC_G = 8
DEC_BUFS = 5
DEC_ROWS = 32
NEG_INF = float("-inf")


def _cparams(sem, vmem_mb):
    return pltpu.CompilerParams(dimension_semantics=sem, vmem_limit_bytes=vmem_mb * 1024 * 1024)


def _dot(a, b):
    return jnp.dot(a, b, preferred_element_type=F32)


def _dot_nt(a, b):
    return lax.dot_general(a, b, (((1,), (1,)), ((), ())), preferred_element_type=F32)


def _layer_norm(x, g, b):
    mu = jnp.mean(x, axis=-1, keepdims=True)
    xc = x - mu
    var = jnp.mean(xc * xc, axis=-1, keepdims=True)
    return xc * lax.rsqrt(var + LN_EPS) * g + b


def _rms_norm(x, g):
    return x * lax.rsqrt(jnp.mean(x * x, axis=-1, keepdims=True) + RMS_EPS) * g


def _rope_slabs(x, c, s_up, s_dn):
    outs = []
    for s in range(x.shape[1] // LANES):
        xs = x[:, LANES * s:LANES * (s + 1)]
        outs.append(xs * c + pltpu.roll(xs, HALF, 1) * s_up + pltpu.roll(xs, LANES - HALF, 1) * s_dn)
    return outs[0] if len(outs) == 1 else jnp.concatenate(outs, axis=1)


def _sub_ln_slab(x, g128):
    lane = lax.broadcasted_iota(jnp.int32, x.shape, 1)
    lo = lane < DIFF_V
    x2 = x * x
    s0 = jnp.sum(jnp.where(lo, x2, 0.0), axis=1, keepdims=True)
    s1 = jnp.sum(jnp.where(lo, 0.0, x2), axis=1, keepdims=True)
    ms = jnp.where(lo, s0, s1) * (1.0 / DIFF_V)
    return x * lax.rsqrt(ms + RMS_EPS) * g128 * (1.0 - LAM_INIT)


def _lambda(dl):
    a = jnp.sum(dl[0:1, :] * dl[1:2, :], axis=1, keepdims=True)
    b = jnp.sum(dl[2:3, :] * dl[3:4, :], axis=1, keepdims=True)
    return jnp.exp(a) - jnp.exp(b) + LAM_INIT


def _front_core(x, cosb, sinb, lng, lnb, win_ref, bg, qn, wuq_ref, kvn, wkv_ref):
    tm = x.shape[0]
    hb = _layer_norm(x, lng, lnb).astype(BF16)

    lane = lax.broadcasted_iota(jnp.int32, (tm, LANES), 1)
    first = (lane & (DIFF_DIM - 1)) < HALF
    zero = jnp.zeros_like(sinb)
    s_up_d = jnp.where(first, zero, sinb)
    s_dn_d = jnp.where(first, -sinb, zero)
    c_q = jnp.where(lane < NOPE, 1.0, cosb)
    s_up_q = jnp.where((lane >= NOPE + HALF) & (lane < NOPE + ROPE), sinb, zero)
    s_dn_q = jnp.where((lane >= NOPE) & (lane < NOPE + HALF), -sinb, zero)
    c_k = jnp.where(lane < ROPE, cosb, zero)
    s_up_k = jnp.where((lane >= HALF) & (lane < ROPE), sinb, zero)
    s_dn_k = jnp.where(lane < HALF, -sinb, zero)

    ya = _dot(hb, win_ref[:, C_CQ:C_DQ])
    cqn = _rms_norm(ya[:, :Q_RANK], qn).astype(BF16)
    q = _rope_slabs(_dot(cqn, wuq_ref[...]), c_q, s_up_q, s_dn_q) * MLA_SCALE
    lat = _rms_norm(ya[:, Q_RANK:], kvn)
    kv = _dot(lat.astype(BF16), wkv_ref[...])
    ykr = _dot(hb, win_ref[:, C_KR:C_END])
    kr = ykr * c_k + pltpu.roll(ykr, HALF, 1) * s_up_k + pltpu.roll(ykr, LANES - HALF, 1) * s_dn_k
    kr_at_rope = pltpu.roll(kr, NOPE, 1)
    k_mla = jnp.concatenate(
        [kv[:, SLAB * s:SLAB * (s + 1)] + kr_at_rope for s in range(HEADS)], axis=1)
    v_mla = kv[:, HEADS * SLAB:]

    yd = _dot(hb, win_ref[:, C_DQ:C_G])
    qd = _rope_slabs(yd[:, 0:512], cosb, s_up_d, s_dn_d) * DIFF_SCALE
    kd = _rope_slabs(yd[:, 512:1024], cosb, s_up_d, s_dn_d)
    vd = yd[:, 1024:1536]
    gates = jax.nn.sigmoid(_dot(hb, win_ref[:, C_G:C_KR]) + bg)
    return dict(q=q, lat=lat, kr=kr, k_mla=k_mla, v_mla=v_mla, qd=qd, kd=kd, vd=vd, gates=gates)


def _front_prompt_kernel(x_ref, cos_ref, sin_ref, lng_ref, lnb_ref, win_ref, bg_ref, qn_ref, wuq_ref,
                         kvn_ref, wkv_ref,
                         lat_o, kr_o, kd_o, vd_o, qm_o, km_o, vm_o, qd_o, kdb_o, vdb_o, g_o):
    r = _front_core(x_ref[...], cos_ref[...], sin_ref[...], lng_ref[...], lnb_ref[...], win_ref,
                    bg_ref[...], qn_ref[...], wuq_ref, kvn_ref[...], wkv_ref)
    lat_o[0, 0] = r["lat"]
    kr_o[0, 0] = r["kr"][:, :ROPE]
    kd_o[0, 0] = r["kd"]
    vd_o[0, 0] = r["vd"]
    qm_o[...] = r["q"].astype(BF16)
    km_o[...] = r["k_mla"].astype(BF16)
    vm_o[...] = r["v_mla"].astype(BF16)
    qd_o[...] = r["qd"].astype(BF16)
    kdb_o[...] = r["kd"].astype(BF16)
    vdb_o[...] = r["vd"].astype(BF16)
    g_o[...] = r["gates"]


def _front_meta_kernel(x_ref, cos_ref, sin_ref, lng_ref, lnb_ref, win_ref, bg_ref, qn_ref, wuq_ref,
                       kvn_ref, wkv_ref, lat_in, kr_in, kd_in, vd_in,
                       lat_o, kr_o, kd_o, vd_o, km_o, vm_o, kdb_o, vdb_o):
    del lat_in, kr_in, kd_in, vd_in
    r = _front_core(x_ref[...], cos_ref[...], sin_ref[...], lng_ref[...], lnb_ref[...], win_ref,
                    bg_ref[...], qn_ref[...], wuq_ref, kvn_ref[...], wkv_ref)
    n = N_META_TOK
    lat_o[...] = r["lat"][:n]
    kr_o[...] = r["kr"][:n, :ROPE]
    kd_o[...] = r["kd"][:n]
    vd_o[...] = r["vd"][:n]
    km_o[...] = r["k_mla"].astype(BF16)
    vm_o[...] = r["v_mla"].astype(BF16)
    kdb_o[...] = r["kd"].astype(BF16)
    vdb_o[...] = r["vd"].astype(BF16)


def _front_sample_kernel(x_ref, cos_ref, sin_ref, lng_ref, lnb_ref, win_ref, bg_ref, qn_ref, wuq_ref,
                         kvn_ref, wkv_ref,
                         lat_o, kr_o, kd_o, vd_o, g_o, qlat_o, qrope_o, qd_o, ss_o):
    r = _front_core(x_ref[...], cos_ref[...], sin_ref[...], lng_ref[...], lnb_ref[...], win_ref,
                    bg_ref[...], qn_ref[...], wuq_ref, kvn_ref[...], wkv_ref)
    nb = x_ref.shape[0]
    lat, kr, kd, q, qd = r["lat"], r["kr"], r["kd"], r["q"], r["qd"]
    lat_o[...] = lat
    kr_o[...] = kr[:, :ROPE]
    kd_o[...] = kd
    vd_o[...] = r["vd"]
    g_o[...] = r["gates"]
    qd_o[...] = qd

    lane = lax.broadcasted_iota(jnp.int32, (nb, LANES), 1)
    kr_at_rope = pltpu.roll(kr, NOPE, 1)
    rope_lanes = (lane >= NOPE) & (lane < NOPE + ROPE)
    rep = lambda col: jnp.broadcast_to(col, (nb, LANES))
    for h in range(HEADS):
        q_slab = q[:, SLAB * h:SLAB * (h + 1)]
        q_lat = _dot_nt(q_slab.astype(BF16), wkv_ref[:, SLAB * h:SLAB * (h + 1)])
        qlat_o[h] = q_lat
        qrope_o[h] = pltpu.roll(q_slab, LANES - NOPE, 1)[:, :ROPE]
        s_h = (jnp.sum(q_lat * lat, axis=1, keepdims=True)
               + jnp.sum(jnp.where(rope_lanes, q_slab * kr_at_rope, 0.0), axis=1, keepdims=True))
        ss_o[h] = rep(s_h)
    lane4 = lax.broadcasted_iota(jnp.int32, (nb, 4 * LANES), 1)
    prod = qd * kd
    for h in range(HEADS):
        for m in range(2):
            lo = h * 2 * DIFF_DIM + m * DIFF_DIM
            seg = (lane4 >= lo) & (lane4 < lo + DIFF_DIM)
            ss_o[HEADS + HEADS * m + h] = rep(jnp.sum(jnp.where(seg, prod, 0.0), axis=1, keepdims=True))
    for c in range(3 * HEADS, DEC_ROWS):
        ss_o[c] = jnp.zeros((nb, LANES), F32)


def _softmax_step(blocks, first, m_scr, l_scr, acc_scr):
    m_cur = None
    for s, _ in blocks:
        m_b = jnp.max(s, axis=1, keepdims=True)
        m_cur = m_b if m_cur is None else jnp.maximum(m_cur, m_b)
    if first:
        m_new = jnp.broadcast_to(m_cur, m_scr.shape)
    else:
        m_prev = m_scr[...]
        m_new = jnp.maximum(m_prev, m_cur)
        corr = jnp.exp2(m_prev - m_new)
    l_cur = None
    pv = None
    for s, v in blocks:
        reps = s.shape[1] // LANES
        p = jnp.exp2(s - (jnp.tile(m_new, (1, reps)) if reps > 1 else m_new))
        l_b = jnp.sum(p, axis=1, keepdims=True)
        pv_b = _dot(p.astype(BF16), v)
        l_cur = l_b if l_cur is None else l_cur + l_b
        pv = pv_b if pv is None else pv + pv_b
    if first:
        l_scr[...] = jnp.broadcast_to(l_cur, l_scr.shape)
        acc_scr[...] = pv
    else:
        l_scr[...] = corr * l_scr[...] + l_cur
        acc_scr[...] = corr * acc_scr[...] + pv
    m_scr[...] = m_new


def _scores(groups, k_at):
    parts = [_dot_nt(q, k_at(ko)) for q, ko in groups]
    return parts[0] if len(parts) == 1 else jnp.concatenate(parts, axis=0)


def _edge_step(qi, tq, tk, groups, k_ref, v_ref, km_ref, vm_ref, m_scr, l_scr, acc_scr):
    rows = sum(q.shape[0] for q, _ in groups)
    ratio = tk // tq

    def edge_blocks(n_rem):
        col = lax.broadcasted_iota(jnp.int32, (rows, LANES), 1)
        s = _scores(groups, lambda ko: km_ref[:, ko:ko + LANES])
        blocks = [(jnp.where(col < N_META_TOK, s, NEG_INF), vm_ref[...])]
        for j in range(n_rem):
            start = pl.multiple_of((qi - n_rem + j) * tq, tq)
            blocks.append((_scores(groups, lambda ko: k_ref[pl.ds(start, tq), ko:ko + LANES]),
                           v_ref[pl.ds(start, tq), :]))
        start = pl.multiple_of(qi * tq, tq)
        row = lax.broadcasted_iota(jnp.int32, (rows, tq), 0)
        colq = lax.broadcasted_iota(jnp.int32, (rows, tq), 1)
        s = _scores(groups, lambda ko: k_ref[pl.ds(start, tq), ko:ko + LANES])
        blocks.append((jnp.where(colq <= (row & (tq - 1)), s, NEG_INF), v_ref[pl.ds(start, tq), :]))
        return blocks

    if ratio == 1:
        _softmax_step(edge_blocks(0), True, m_scr, l_scr, acc_scr)
    else:
        for n_rem in range(ratio):
            @pl.when(qi % ratio == n_rem)
            def _():
                _softmax_step(edge_blocks(n_rem), True, m_scr, l_scr, acc_scr)


def _full_step(c, tk, groups, k_ref, v_ref, m_scr, l_scr, acc_scr):
    start = pl.multiple_of(c * tk, tk)
    s = _scores(groups, lambda ko: k_ref[pl.ds(start, tk), ko:ko + LANES])
    _softmax_step([(s, v_ref[pl.ds(start, tk), :])], False, m_scr, l_scr, acc_scr)


def _causal_sweep(qi, tq, tk, groups, k_ref, v_ref, km_ref, vm_ref, m_scr, l_scr, acc_scr):
    _edge_step(qi, tq, tk, groups, k_ref, v_ref, km_ref, vm_ref, m_scr, l_scr, acc_scr)

    def full_chunk(c, carry):
        _full_step(c, tk, groups, k_ref, v_ref, m_scr, l_scr, acc_scr)
        return carry

    lax.fori_loop(0, qi // (tk // tq), full_chunk, 0)


def _attn_mla_kernel(q_ref, k_ref, v_ref, km_ref, vm_ref, o_ref, m_scr, l_scr, acc_scr):
    qi = pl.program_id(2)
    tq = q_ref.shape[0]
    groups = [(q_ref[:, SLAB * hh:SLAB * (hh + 1)], SLAB * hh) for hh in range(2)]
    _causal_sweep(qi, tq, TK_MLA, groups, k_ref, v_ref, km_ref, vm_ref, m_scr, l_scr, acc_scr)
    lane = lax.broadcasted_iota(jnp.int32, (tq, LANES), 1)
    o = acc_scr[...] / l_scr[...]
    o_ref[...] = jnp.where(lane < MLA_V, o[0:tq], o[tq:2 * tq]).astype(o_ref.dtype)


def _attn_diff_kernel(q_ref, k_ref, v_ref, km_ref, vm_ref, dl_ref, g_ref, o_ref, qs_scr, m_scr, l_scr, acc_scr):
    qi = pl.program_id(2)
    tq = q_ref.shape[0]
    q = q_ref[...]
    lane = lax.broadcasted_iota(jnp.int32, (tq, LANES), 1)
    zero = jnp.zeros_like(q)
    for j in range(4):
        qs_scr[j * tq:(j + 1) * tq, :] = jnp.where((lane >= DIFF_DIM * j) & (lane < DIFF_DIM * (j + 1)), q, zero)
    _causal_sweep(qi, tq, TK_DIFF, [(qs_scr[...], 0)], k_ref, v_ref, km_ref, vm_ref, m_scr, l_scr, acc_scr)
    lam = _lambda(dl_ref[...])
    o = acc_scr[...] / l_scr[...]
    o0 = o[0:tq] - lam * o[tq:2 * tq]
    o1 = o[2 * tq:3 * tq] - lam * o[3 * tq:4 * tq]
    o_ref[...] = _sub_ln_slab(jnp.where(lane < DIFF_V, o0, o1), g_ref[...]).astype(o_ref.dtype)


def _decode_step(step, n_steps, n_t, pt_ref, qlat_ref, qrope_ref, qd_ref, ss_ref, lats_ref, vds_ref, dl_ref,
                 wuv_ref, lat_hbm, kr_hbm, kt_hbm, vt_hbm, om_ref, od_ref,
                 lat_buf, kr_buf, kt_buf, vt_buf, sem, qcol, ml, ll, accl, md, ld, accd):
    g_n = DEC_G
    b = step // n_t
    t = lax.rem(step, n_t)
    r = lax.rem(b, SUBLANES)
    n_maps = 2 * HEADS

    def page_copies(src_step, slot_, g):
        page = pt_ref[src_step * g_n + g]
        return (pltpu.make_async_copy(lat_hbm.at[0, page], lat_buf.at[slot_, g], sem.at[slot_, 0]),
                pltpu.make_async_copy(kr_hbm.at[0, page], kr_buf.at[slot_, g], sem.at[slot_, 1]),
                pltpu.make_async_copy(kt_hbm.at[0, page], kt_buf.at[slot_, g], sem.at[slot_, 2]),
                pltpu.make_async_copy(vt_hbm.at[0, page], vt_buf.at[slot_, g], sem.at[slot_, 3]))

    ahead = DEC_BUFS - 1
    slot = lax.rem(step, DEC_BUFS)
    nxt_step = lax.rem(step + ahead, n_steps)
    nxt_slot = lax.rem(step + ahead, DEC_BUFS)

    @pl.when(step == 0)
    def _():
        for s0 in range(ahead):
            for g in range(g_n):
                for c in page_copies(s0, s0, g):
                    c.start()

    for g in range(g_n):
        for c in page_copies(step, slot, g):
            c.wait()
    for g in range(g_n):
        for c in page_copies(nxt_step, nxt_slot, g):
            c.start()

    lat_refs = [lat_buf.at[slot, g] for g in range(g_n)]
    kr_refs = [kr_buf.at[slot, g] for g in range(g_n)]
    kt_refs = [kt_buf.at[slot, g] for g in range(g_n)]
    vt_refs = [vt_buf.at[slot, g] for g in range(g_n)]

    @pl.when(t == 0)
    def _():
        qrow = qd_ref[pl.ds(r, 1), :]
        for s in range(HEADS // 2):
            cols = jnp.broadcast_to(qrow[:, LANES * s:LANES * (s + 1)], (LANES, LANES)).T
            qcol[2 * s] = cols[0:2 * DIFF_DIM]
            qcol[2 * s + 1] = cols[2 * DIFF_DIM:]
        ss = ss_ref[...]
        ml[...] = ss[0:HEADS]
        md[...] = ss[HEADS:HEADS + n_maps]
        ll[...] = jnp.zeros(ll.shape, F32)
        ld[...] = jnp.zeros(ld.shape, F32)
        accl[...] = jnp.zeros(accl.shape, F32)
        accd[...] = jnp.zeros(accd.shape, F32)

    qlat = qlat_ref[...].astype(BF16)
    qrope = qrope_ref[...].astype(BF16)
    lat_b = [ref[...].astype(BF16) for ref in lat_refs]
    s_l = [_dot_nt(qlat, lat_b[g]) + _dot(qrope, kr_refs[g][...].astype(BF16)) for g in range(g_n)]
    mx = s_l[0]
    for g in range(1, g_n):
        mx = jnp.maximum(mx, s_l[g])
    m_old = ml[...]
    m_new = jnp.maximum(m_old, jnp.max(mx, axis=1, keepdims=True))
    corr = jnp.exp2(m_old - m_new)
    psum = jnp.zeros((HEADS, LANES), F32)
    o_l = jnp.zeros((HEADS, KV_RANK), F32)
    for g in range(g_n):
        p = jnp.exp2(s_l[g] - m_new)
        psum = psum + p
        o_l = o_l + _dot(p.astype(BF16), lat_b[g])
    ll[...] = ll[...] * corr + jnp.sum(psum, axis=1, keepdims=True)
    accl[...] = accl[...] * jnp.tile(corr, (1, KV_RANK // LANES)) + o_l
    ml[...] = m_new

    for h in range(HEADS):
        qc = qcol[h]
        rows = ([], [])
        for g in range(g_n):
            prod = kt_refs[g][h] * qc
            rows[0].append(jnp.sum(prod[0:DIFF_DIM], axis=0, keepdims=True))
            rows[1].append(jnp.sum(prod[DIFF_DIM:], axis=0, keepdims=True))
        for m in range(2):
            idx = HEADS * m + h
            s_d = jnp.concatenate(rows[m], axis=0)
            m_old = md[idx:idx + 1, :]
            m_new = jnp.maximum(
                m_old, jnp.max(jnp.max(s_d, axis=1, keepdims=True), axis=0, keepdims=True))
            corr = jnp.exp2(m_old - m_new)
            p_d = jnp.exp2(s_d - m_new)
            ld[idx:idx + 1, :] = ld[idx:idx + 1, :] * corr + jnp.sum(
                jnp.sum(p_d, axis=1, keepdims=True), axis=0, keepdims=True)
            acc = accd[idx] * corr
            for g in range(g_n):
                acc = acc + vt_refs[g][h] * p_d[g:g + 1, :]
            accd[idx] = acc
            md[idx:idx + 1, :] = m_new

    @pl.when(t == n_t - 1)
    def _():
        ss = ss_ref[...]
        w_self = jnp.exp2(ss[0:HEADS] - ml[...])
        l_tot = ll[...] + w_self
        reps = KV_RANK // LANES
        o_lat = (accl[...] + jnp.tile(w_self, (1, reps)) * lats_ref[pl.ds(r, 1), :]) / jnp.tile(l_tot, (1, reps))
        om_full = _dot(o_lat.astype(BF16), wuv_ref[...])
        row8 = lax.broadcasted_iota(jnp.int32, (HEADS, MLA_V), 0)
        om = jnp.zeros((HEADS, MLA_V), F32)
        for h in range(HEADS):
            om = jnp.where(row8 == h, om_full[:, MLA_V * h:MLA_V * (h + 1)], om)
        om_ref[...] = om

        lane = lax.broadcasted_iota(jnp.int32, (DIFF_V, LANES), 1)
        cols = jnp.zeros((DIFF_V, LANES), F32)
        for idx in range(n_maps):
            cols = jnp.where(lane == idx, jnp.sum(accd[idx], axis=1, keepdims=True), cols)
        o_rows = jnp.concatenate([cols, jnp.zeros((LANES - DIFF_V, LANES), F32)], axis=0).T
        w_d = jnp.exp2(ss[HEADS:HEADS + n_maps] - md[...])
        l_d = ld[...] + w_d
        vd_self = vds_ref[...]
        o1 = (o_rows[0:HEADS, 0:DIFF_V] + w_d[0:HEADS, 0:DIFF_V] * vd_self) / l_d[0:HEADS, 0:DIFF_V]
        o2 = (o_rows[HEADS:n_maps, 0:DIFF_V] + w_d[HEADS:, 0:DIFF_V] * vd_self) / l_d[HEADS:, 0:DIFF_V]
        od_ref[...] = o1 - _lambda(dl_ref[...]) * o2

    @pl.when(step == n_steps - 1)
    def _():
        for k in range(ahead):
            for g in range(g_n):
                for c in page_copies(k, (n_steps + k) % DEC_BUFS, g):
                    c.wait()


N_DEC_IN, N_BACK_IN, N_DEC_SCRATCH = 12, 16, 12


def _decode_kernel(n_steps, n_t, pt_ref, *refs):
    _decode_step(pl.program_id(0) * n_t + pl.program_id(1), n_steps, n_t, pt_ref, *refs)


def _decode_back_kernel(n_steps, n_t, pt_ref, *refs):
    dec_in = refs[:N_DEC_IN]
    back_in = refs[N_DEC_IN:N_DEC_IN + N_BACK_IN]
    om_ref, od_ref, y_ref = refs[N_DEC_IN + N_BACK_IN:N_DEC_IN + N_BACK_IN + 3]
    scratch = refs[N_DEC_IN + N_BACK_IN + 3:]
    t = pl.program_id(1)
    _decode_step(pl.program_id(0) * n_t + t, n_steps, n_t, pt_ref, *dec_in, om_ref, od_ref, *scratch)

    @pl.when(t == n_t // 2)
    def _():
        _back_kernel(False, *back_in, y_ref)


def _back_kernel(raw_diff, x_ref, om_ref, od_ref, g_ref, lng_ref, lnb_ref, subln_ref, wbm_ref, wbd_ref,
                 wout_ref, ln1g_ref, ln1b_ref, wup_ref, wdn_ref, ln2g_ref, ln2b_ref, y_ref):
    h = _layer_norm(x_ref[...], lng_ref[...], lnb_ref[...])
    od = od_ref[...]
    if raw_diff:
        g128 = subln_ref[...]
        od = jnp.concatenate(
            [_sub_ln_slab(od[:, LANES * s:LANES * (s + 1)], g128) for s in range(od.shape[1] // LANES)], axis=1)
    a = _dot(om_ref[...].astype(BF16), wbm_ref[...])
    b = _dot(od.astype(BF16), wbd_ref[...])
    g = g_ref[...]
    mix = _dot((g[:, :D_MODEL] * a + g[:, D_MODEL:] * b).astype(BF16), wout_ref[...])
    x1 = _layer_norm(DEEPNORM_ALPHA * h + mix, ln1g_ref[...], ln1b_ref[...])
    up = jnp.maximum(_dot(x1.astype(BF16), wup_ref[...]), 0.0)
    f = _dot((up * up).astype(BF16), wdn_ref[...])
    y_ref[...] = _layer_norm(DEEPNORM_ALPHA * x1 + f, ln2g_ref[...], ln2b_ref[...])


def _const_spec(shape):
    nd = len(shape)
    return pl.BlockSpec(shape, lambda *_: (0,) * nd, pipeline_mode=pl.Buffered(1))


def _prompt_attention(q_mla, k_mla, v_mla, km_meta, vm_meta, q_diff, k_diff, v_diff, kd_meta, vd_meta, dl, subln):
    n_b, seq, _ = q_mla.shape
    sem = ("arbitrary", "arbitrary", "arbitrary")
    q_spec = lambda tq, wdt: pl.BlockSpec((None, tq, wdt), lambda b, p, i: (b, i, p))
    kv_spec = lambda wdt: pl.BlockSpec((None, seq, wdt), lambda b, p, i: (b, 0, p))
    meta_spec = lambda wdt: pl.BlockSpec((LANES, wdt), lambda b, p, i: (0, p))
    o_shape = jax.ShapeDtypeStruct((n_b, seq, HEADS * MLA_V), BF16)
    state = lambda rows: [pltpu.VMEM((rows, LANES), F32) for _ in range(3)]
    tq_m, tq_d = min(TQ_MLA, seq), min(TQ_DIFF, seq)

    o_mla = pl.pallas_call(
        _attn_mla_kernel,
        grid=(n_b, HEADS // 2, seq // tq_m),
        in_specs=[q_spec(tq_m, 2 * SLAB), kv_spec(2 * SLAB), kv_spec(LANES), meta_spec(2 * SLAB), meta_spec(LANES)],
        out_specs=q_spec(tq_m, LANES),
        out_shape=o_shape,
        scratch_shapes=state(2 * tq_m),
        compiler_params=_cparams(sem, 48),
        name="attn_mla",
    )(q_mla, k_mla, v_mla, km_meta, vm_meta)

    o_diff = pl.pallas_call(
        _attn_diff_kernel,
        grid=(n_b, HEADS // 2, seq // tq_d),
        in_specs=[q_spec(tq_d, LANES), kv_spec(LANES), kv_spec(LANES), meta_spec(LANES), meta_spec(LANES),
                  pl.BlockSpec(dl.shape, lambda b, p, i: (0, 0)), pl.BlockSpec(subln.shape, lambda b, p, i: (0, 0))],
        out_specs=q_spec(tq_d, LANES),
        out_shape=o_shape,
        scratch_shapes=[pltpu.VMEM((4 * tq_d, LANES), BF16)] + state(4 * tq_d),
        compiler_params=_cparams(sem, 48),
        name="attn_diff",
    )(q_diff, k_diff, v_diff, kd_meta, vd_meta, dl, subln)
    return o_mla, o_diff


def _back_rows_per_sequence(n_b, seq, n_dec):
    rows = (n_b * seq) // n_dec
    ok = (n_b * seq) % n_dec == 0 and rows >= LANES and rows % LANES == 0 and seq % rows == 0
    return rows if ok else 0


def _decode_attention(page_table, qlat_b, qrope_b, qd_s, ss_b, lat_s, vds, dl, wuv_b, cache_lat, kr_t, kd_t, vd_t,
                      back=None):
    n_dec, n_pages = page_table.shape
    n_t = n_pages // DEC_G
    pt_flat = page_table.reshape(-1).astype(jnp.int32)
    per_seq = lambda rows, wdt: pl.BlockSpec((None, rows, wdt), lambda b, t, pt: (b, 0, 0))
    group8 = lambda wdt: pl.BlockSpec((SUBLANES, wdt), lambda b, t, pt: (b // SUBLANES, 0))
    const2 = lambda a: pl.BlockSpec(a.shape, lambda b, t, pt: (0, 0))
    any_spec = pl.BlockSpec(memory_space=pl.ANY)
    dec_out_spec = pl.BlockSpec((None, HEADS, MLA_V), lambda b, t, pt: (b, 0, 0))
    body, back_args, back_specs = _decode_kernel, (), []
    out_specs = [dec_out_spec, dec_out_spec]
    out_shape = [jax.ShapeDtypeStruct((n_dec, HEADS, MLA_V), F32)] * 2
    if back is not None:
        x, o_mla, o_diff, gates, params = back
        n_b, seq, d = x.shape
        rows = _back_rows_per_sequence(n_b, seq, n_dec)
        per = seq // rows
        tile = lambda wdt: pl.BlockSpec((None, rows, wdt), lambda b, t, pt: (b // per, b % per, 0))
        body = _decode_back_kernel
        back_args = (x, o_mla, o_diff, gates) + tuple(params)
        back_specs = [tile(d), tile(o_mla.shape[2]), tile(o_diff.shape[2]), tile(gates.shape[2])] + [
            _const_spec(p.shape) for p in params]
        out_specs = out_specs + [tile(d)]
        out_shape = out_shape + [jax.ShapeDtypeStruct((n_b, seq, d), F32)]
    return pl.pallas_call(
        functools.partial(body, n_dec * n_t, n_t),
        grid_spec=pltpu.PrefetchScalarGridSpec(
            num_scalar_prefetch=1,
            grid=(n_dec, n_t),
            in_specs=[per_seq(HEADS, KV_RANK), per_seq(HEADS, ROPE), group8(512), per_seq(DEC_ROWS, LANES),
                      group8(KV_RANK), per_seq(HEADS, DIFF_V), const2(dl), const2(wuv_b)]
            + [any_spec] * 4 + back_specs,
            out_specs=out_specs,
            scratch_shapes=[pltpu.VMEM((DEC_BUFS, DEC_G, PAGE, KV_RANK), F32),
                            pltpu.VMEM((DEC_BUFS, DEC_G, ROPE, PAGE), F32),
                            pltpu.VMEM((DEC_BUFS, DEC_G, HEADS, 2 * DIFF_DIM, PAGE), F32),
                            pltpu.VMEM((DEC_BUFS, DEC_G, HEADS, DIFF_V, PAGE), F32),
                            pltpu.SemaphoreType.DMA((DEC_BUFS, 4)),
                            pltpu.VMEM((HEADS, 2 * DIFF_DIM, LANES), F32),
                            pltpu.VMEM((HEADS, LANES), F32), pltpu.VMEM((HEADS, LANES), F32),
                            pltpu.VMEM((HEADS, KV_RANK), F32),
                            pltpu.VMEM((2 * HEADS, LANES), F32), pltpu.VMEM((2 * HEADS, LANES), F32),
                            pltpu.VMEM((2 * HEADS, DIFF_V, LANES), F32)]),
        out_shape=out_shape,
        compiler_params=_cparams(("arbitrary", "arbitrary"), 57),
        name="decode_attn",
    )(pt_flat, qlat_b, qrope_b, qd_s, ss_b, lat_s, vds, dl, wuv_b, cache_lat, kr_t, kd_t, vd_t, *back_args)


def _rope_tables(pos):
    inv = ROPE_THETA ** (-jnp.arange(HALF, dtype=F32) / HALF)
    ang = pos.astype(F32)[:, None] * inv[None, :]
    reps = LANES // HALF
    return jnp.tile(jnp.cos(ang), (1, reps)), jnp.tile(jnp.sin(ang), (1, reps))


def kernel(x_prompt, x_sample, cache_mla_latent, cache_mla_krope, cache_diff_k, cache_diff_v, page_table,
           meta_tokens, ln_in_g, ln_in_b, w_in, b_gate, mla_q_norm, w_uq, mla_kv_norm, w_uk, w_uv,
           diff_lambda, diff_subln, w_br_mla, w_br_diff, w_out, ln1_g, ln1_b, w_up, w_down, ln2_g, ln2_b):
    n_b, seq, d = x_prompt.shape
    n_dec = x_sample.shape[0]
    n_pages = page_table.shape[1]
    l_all = seq + N_META_TOK
    nq = seq // TM
    layer = 0

    w = w_in[layer]
    win = jnp.concatenate(
        [w[:, 0:512], w[:, 544:4128], w[:, 512:544], jnp.zeros((d, C_END - C_KR - ROPE), F32)],
        axis=1).astype(BF16)
    wuq = jnp.pad(w_uq[layer].reshape(Q_RANK, HEADS, NOPE + ROPE),
                  ((0, 0), (0, 0), (0, SLAB - NOPE - ROPE))).reshape(Q_RANK, HEADS * SLAB).astype(BF16)
    wuk = jnp.pad(w_uk[layer], ((0, 0), (0, 0), (0, SLAB - NOPE))).reshape(KV_RANK, HEADS * SLAB)
    wuv = w_uv[layer].reshape(KV_RANK, HEADS * MLA_V)
    wkv = jnp.concatenate([wuk, wuv], axis=1).astype(BF16)
    wuv_b = wuv.astype(BF16)
    row = lambda v: v.reshape(1, -1)
    lng, lnb = row(ln_in_g), row(ln_in_b)
    bg, qn, kvn = row(b_gate[layer]), row(mla_q_norm[layer]), row(mla_kv_norm[layer])
    subln = row(jnp.tile(diff_subln[layer], LANES // DIFF_V))
    dl = diff_lambda[layer].astype(F32)
    front_params = (lng, lnb, win, bg, qn, wuq, kvn, wkv)
    front_param_specs = [_const_spec(p.shape) for p in front_params]

    cos_p, sin_p = _rope_tables(jnp.arange(N_META_TOK, l_all))
    cos_m, sin_m = _rope_tables(jnp.arange(LANES))
    cos_s, sin_s = _rope_tables(jnp.full((n_dec,), n_pages * PAGE))

    tok = lambda wdt, dt: jax.ShapeDtypeStruct((n_b, seq, wdt), dt)
    cache_shape = lambda wdt: jax.ShapeDtypeStruct((DEPTH, n_b, l_all, wdt), F32)
    tok_spec = lambda wdt: pl.BlockSpec((None, TM, wdt), lambda b, i: (b, i, 0))
    cache_spec = lambda wdt: pl.BlockSpec((pl.Element(1), pl.Element(1), pl.Element(TM), pl.Element(wdt)),
                                          lambda b, i: (0, b, pl.multiple_of(N_META_TOK + i * TM, N_META_TOK), 0))
    (lat_p, kr_p, kd_p, vd_p, q_mla, k_mla, v_mla, q_diff, k_diff, v_diff, gates_p) = pl.pallas_call(
        _front_prompt_kernel,
        grid=(n_b, nq),
        in_specs=[tok_spec(d), pl.BlockSpec((TM, LANES), lambda b, i: (i, 0)),
                  pl.BlockSpec((TM, LANES), lambda b, i: (i, 0))] + front_param_specs,
        out_specs=[cache_spec(KV_RANK), cache_spec(ROPE), cache_spec(512), cache_spec(512),
                   tok_spec(HEADS * SLAB), tok_spec(HEADS * SLAB), tok_spec(512),
                   tok_spec(512), tok_spec(512), tok_spec(512), tok_spec(2 * d)],
        out_shape=[cache_shape(KV_RANK), cache_shape(ROPE), cache_shape(512), cache_shape(512),
                   tok(HEADS * SLAB, BF16), tok(HEADS * SLAB, BF16), tok(512, BF16),
                   tok(512, BF16), tok(512, BF16), tok(512, BF16), tok(2 * d, F32)],
        compiler_params=_cparams(("arbitrary", "arbitrary"), 48),
        name="front_prompt",
    )(x_prompt, cos_p, sin_p, *front_params)

    meta_pad = jnp.pad(meta_tokens.astype(F32), ((0, LANES - N_META_TOK), (0, 0)))
    any_spec = pl.BlockSpec(memory_space=pl.ANY)
    meta_cache_spec = lambda wdt: pl.BlockSpec((None, None, N_META_TOK, wdt), lambda b: (0, b, 0, 0))
    mrow = lambda wdt: jax.ShapeDtypeStruct((LANES, wdt), BF16)
    mrow_spec = lambda wdt: pl.BlockSpec((LANES, wdt), lambda b: (0, 0))
    n_fp = 3 + len(front_params)
    (lat_p, kr_p, kd_p, vd_p, km_meta, vm_meta, kd_meta, vd_meta) = pl.pallas_call(
        _front_meta_kernel,
        grid=(n_b,),
        in_specs=[_const_spec((LANES, d)), _const_spec((LANES, LANES)), _const_spec((LANES, LANES))]
        + front_param_specs + [any_spec] * 4,
        out_specs=[meta_cache_spec(KV_RANK), meta_cache_spec(ROPE), meta_cache_spec(512), meta_cache_spec(512),
                   mrow_spec(HEADS * SLAB), mrow_spec(512), mrow_spec(512), mrow_spec(512)],
        out_shape=[cache_shape(KV_RANK), cache_shape(ROPE), cache_shape(512), cache_shape(512),
                   mrow(HEADS * SLAB), mrow(512), mrow(512), mrow(512)],
        input_output_aliases={n_fp: 0, n_fp + 1: 1, n_fp + 2: 2, n_fp + 3: 3},
        compiler_params=_cparams(("arbitrary",), 40),
        name="front_meta",
    )(meta_pad, cos_m, sin_m, *front_params, lat_p, kr_p, kd_p, vd_p)

    xs = x_sample.reshape(n_dec, d)
    sds = lambda shape, dt: jax.ShapeDtypeStruct(shape, dt)
    full = lambda shape: pl.BlockSpec(shape, lambda i: (0,) * len(shape))
    s_shapes = [(n_dec, KV_RANK), (n_dec, ROPE), (n_dec, 512), (n_dec, 512), (n_dec, 2 * d),
                (HEADS, n_dec, KV_RANK), (HEADS, n_dec, ROPE), (n_dec, 512), (DEC_ROWS, n_dec, LANES)]
    (lat_s, kr_s, kd_s, vd_s, gates_s, qlat_s, qrope_s, qd_s, ss_s) = pl.pallas_call(
        _front_sample_kernel,
        grid=(1,),
        in_specs=[_const_spec((n_dec, d)), _const_spec((n_dec, LANES)), _const_spec((n_dec, LANES))]
        + front_param_specs,
        out_specs=[full(s) for s in s_shapes],
        out_shape=[sds(s, F32) for s in s_shapes],
        compiler_params=_cparams(("arbitrary",), 40),
        name="front_sample",
    )(xs, cos_s, sin_s, *front_params)

    o_mla, o_diff = _prompt_attention(q_mla, k_mla, v_mla, km_meta, vm_meta,
                                      q_diff, k_diff, v_diff, kd_meta, vd_meta, dl, subln)

    kr_t = jnp.transpose(cache_mla_krope, (0, 1, 3, 2))
    kd_t = jnp.transpose(cache_diff_k, (0, 1, 3, 4, 2))
    vd_t = jnp.transpose(cache_diff_v, (0, 1, 3, 4, 2))
    qlat_b = jnp.transpose(qlat_s, (1, 0, 2))
    qrope_b = jnp.transpose(qrope_s, (1, 0, 2))
    ss_b = jnp.transpose(ss_s, (1, 0, 2))
    back_params = (lng, lnb, subln, w_br_mla[layer].astype(BF16), w_br_diff[layer].astype(BF16),
                   w_out[layer].astype(BF16), row(ln1_g[layer]), row(ln1_b[layer]), w_up[layer].astype(BF16),
                   w_down[layer].astype(BF16), row(ln2_g[layer]), row(ln2_b[layer]))
    back_param_specs = [_const_spec(p.shape) for p in back_params]
    dec_args = (page_table, qlat_b, qrope_b, qd_s, ss_b, lat_s, vd_s.reshape(n_dec, HEADS, DIFF_V), dl, wuv_b,
                cache_mla_latent, kr_t, kd_t, vd_t)
    if _back_rows_per_sequence(n_b, seq, n_dec):
        om_s, od_s, y_prompt = _decode_attention(*dec_args, back=(x_prompt, o_mla, o_diff, gates_p, back_params))
    else:
        om_s, od_s = _decode_attention(*dec_args)
        y_prompt = pl.pallas_call(
            functools.partial(_back_kernel, False),
            grid=(n_b, nq),
            in_specs=[tok_spec(d), tok_spec(512), tok_spec(512), tok_spec(2 * d)] + back_param_specs,
            out_specs=tok_spec(d),
            out_shape=jax.ShapeDtypeStruct((n_b, seq, d), F32),
            compiler_params=_cparams(("arbitrary", "arbitrary"), 56),
            name="back_prompt",
        )(x_prompt, o_mla, o_diff, gates_p, *back_params)

    y_sample = pl.pallas_call(
        functools.partial(_back_kernel, True),
        grid=(1,),
        in_specs=[full((n_dec, d)), full((n_dec, 512)), full((n_dec, 512)), full((n_dec, 2 * d))]
        + back_param_specs,
        out_specs=full((n_dec, d)),
        out_shape=jax.ShapeDtypeStruct((n_dec, d), F32),
        compiler_params=_cparams(("arbitrary",), 56),
        name="back_sample",
    )(xs, om_s.reshape(n_dec, 512), od_s.reshape(n_dec, 512), gates_s, *back_params)

    head_shape = lambda a, lead, v: a.reshape(lead + (HEADS, v))
    return (y_prompt, y_sample.reshape(n_dec, 1, d),
            lat_p, kr_p,
            head_shape(kd_p, (DEPTH, n_b, l_all), 2 * DIFF_DIM), head_shape(vd_p, (DEPTH, n_b, l_all), DIFF_V),
            lat_s.reshape(DEPTH, n_dec, 1, KV_RANK), kr_s.reshape(DEPTH, n_dec, 1, ROPE),
            head_shape(kd_s, (DEPTH, n_dec, 1), 2 * DIFF_DIM), head_shape(vd_s, (DEPTH, n_dec, 1), DIFF_V))
```
